```python
import jax
import jax.numpy as jnp
from jax import lax
import numpy as np

D_MODEL = 2048
BATCH = 2
SEQ = 4096
DEPTH = 4

EPS = 1e-6
BRANCH_WIDTH = 1024
N_BRANCH = 3
A_HEADS = 8
A_KEY = 128
A_VAL = BRANCH_WIDTH // A_HEADS
A_FDIM = A_HEADS * A_KEY
A_CHUNK = 64
B_HEADS = 8
B_HDIM = BRANCH_WIDTH // B_HEADS
B_CONV = 4
B_C = 8.0
C_HDIM = 64
C_HEADS = BRANCH_WIDTH // C_HDIM
C_GROUPS = 2
C_STATE = 128
C_CONV = 4
C_CHUNK = 128
C_CONV_DIM = BRANCH_WIDTH + 2 * C_GROUPS * C_STATE
SPLITS = (A_FDIM, A_FDIM, BRANCH_WIDTH, BRANCH_WIDTH, BRANCH_WIDTH, BRANCH_WIDTH, BRANCH_WIDTH, C_CONV_DIM, C_HEADS, N_BRANCH * D_MODEL)
IN_COLS = sum(SPLITS)

kernel_name = 'hybrid_hgrn2_rglru_ssd_gated_merge'


def rmsnorm(x, w):
    xf = x.astype(jnp.float32)
    y = xf * lax.rsqrt(jnp.mean(xf * xf, axis=-1, keepdims=True) + EPS)
    return (y * w.astype(jnp.float32)).astype(x.dtype)


def causal_depthwise_conv(x, w, b):
    width, ch = w.shape
    y = lax.conv_general_dilated(x, w[:, None, :].astype(x.dtype), window_strides=(1,),
                                 padding=[(width - 1, 0)], dimension_numbers=('NWC', 'WIO', 'NWC'),
                                 feature_group_count=ch)
    return y + b.astype(x.dtype)


def hgrn2_mixer(q, f_pre, i, lb):
    bsz, s, _ = q.shape
    n = s // A_CHUNK
    lbf = lb.astype(jnp.float32)
    log_f = jnp.logaddexp(jnp.log(lbf), jnp.log1p(-lbf) + jax.nn.log_sigmoid(f_pre.astype(jnp.float32)))
    k = -jnp.expm1(log_f)

    def to_chunks(t, d):
        return t.astype(jnp.float32).reshape(bsz, n, A_CHUNK, A_HEADS, d).transpose(1, 0, 3, 2, 4)

    qc, kc, gc, vc = to_chunks(q, A_KEY), to_chunks(k, A_KEY), to_chunks(log_f, A_KEY), to_chunks(i, A_VAL)
    causal = jnp.tril(jnp.ones((A_CHUNK, A_CHUNK), dtype=bool))

    def step(state, inp):
        qb, kb, gb, vb = inp
        b = jnp.cumsum(gb, axis=-2)
        diff = b[:, :, :, None, :] - b[:, :, None, :, :]
        decay = jnp.exp(jnp.where(causal[:, :, None], diff, -jnp.inf))
        scores = jnp.einsum('bhtk,bhsk,bhtsk->bhts', qb, kb, decay)
        o = (jnp.einsum('bhts,bhsv->bhtv', scores, vb)
             + jnp.einsum('bhtk,bhkv->bhtv', qb * jnp.exp(b), state))
        b_last = b[:, :, -1:, :]
        state = (jnp.exp(b_last[:, :, 0, :, None]) * state
                 + jnp.einsum('bhsk,bhsv->bhkv', kb * jnp.exp(b_last - b), vb))
        return state, o

    s0 = jnp.zeros((bsz, A_HEADS, A_KEY, A_VAL), jnp.float32)
    _, o = lax.scan(step, s0, (qc, kc, gc, vc))
    return o.transpose(1, 0, 3, 2, 4).reshape(bsz, s, A_HEADS, A_VAL)


def rglru_mixer(x, conv_w, conv_b, wa, ba, wx, bx, lam):
    xc = causal_depthwise_conv(x, conv_w, conv_b)
    bsz, s, _ = xc.shape
    xh = xc.astype(jnp.float32).reshape(bsz, s, B_HEADS, B_HDIM)
    r = jax.nn.sigmoid(jnp.einsum('bshi,hij->bshj', xh, wa.astype(jnp.float32)) + ba).reshape(bsz, s, BRANCH_WIDTH)
    ig = jax.nn.sigmoid(jnp.einsum('bshi,hij->bshj', xh, wx.astype(jnp.float32)) + bx).reshape(bsz, s, BRANCH_WIDTH)
    log_a = -B_C * r * jax.nn.softplus(-lam.astype(jnp.float32))
    a = jnp.exp(log_a)
    u = jnp.sqrt(-jnp.expm1(2.0 * log_a)) * (ig * xh.reshape(bsz, s, BRANCH_WIDTH))

    def combine(left, right):
        a1, b1 = left
        a2, b2 = right
        return a1 * a2, a2 * b1 + b2

    _, h = lax.associative_scan(combine, (a, u), axis=1)
    return h


def segsum(x):
    t = x.shape[-1]
    cs = jnp.cumsum(x, axis=-1)
    mask = jnp.tril(jnp.ones((t, t), dtype=bool))
    return jnp.where(mask, cs[..., :, None] - cs[..., None, :], -jnp.inf)


def ssd_mixer(xbc_pre, dt_pre, conv_w, conv_b, dt_bias, a_log, d_skip):
    xbc = jax.nn.silu(causal_depthwise_conv(xbc_pre, conv_w, conv_b)).astype(jnp.float32)
    bsz, s, _ = xbc.shape
    n = s // C_CHUNK
    rep = C_HEADS // C_GROUPS
    xs, bm, cm = jnp.split(xbc, [BRANCH_WIDTH, BRANCH_WIDTH + C_GROUPS * C_STATE], axis=-1)
    xs = xs.reshape(bsz, n, C_CHUNK, C_GROUPS, rep, C_HDIM)
    bm = bm.reshape(bsz, n, C_CHUNK, C_GROUPS, C_STATE)
    cm = cm.reshape(bsz, n, C_CHUNK, C_GROUPS, C_STATE)
    dt = jax.nn.softplus(dt_pre.astype(jnp.float32) + dt_bias.astype(jnp.float32))
    a = -jnp.exp(a_log.astype(jnp.float32))
    dtc = dt.reshape(bsz, n, C_CHUNK, C_GROUPS, rep)
    da = (dtc * a.reshape(C_GROUPS, rep)).transpose(0, 3, 4, 1, 2)
    xdt = xs * dtc[..., None]
    cum = jnp.cumsum(da, axis=-1)
    lmat = jnp.exp(segsum(da))
    cb = jnp.einsum('bclgn,bcsgn->bgcls', cm, bm)
    y_diag = jnp.einsum('bgcls,bgrcls,bcsgrp->bclgrp', cb, lmat, xdt)
    decay_states = jnp.exp(cum[..., -1:] - cum)
    states = jnp.einsum('bclgn,bgrcl,bclgrp->bcgrpn', bm, decay_states, xdt)
    chunk_decay = jnp.exp(segsum(jnp.pad(cum[..., -1], [(0, 0), (0, 0), (0, 0), (1, 0)])))
    states = jnp.concatenate([jnp.zeros_like(states[:, :1]), states], axis=1)
    states = jnp.einsum('bgrzc,bcgrpn->bzgrpn', chunk_decay, states)[:, :-1]
    y_off = jnp.einsum('bclgn,bcgrpn,bgrcl->bclgrp', cm, states, jnp.exp(cum))
    y = y_diag + y_off + d_skip.astype(jnp.float32).reshape(C_GROUPS, rep)[:, :, None] * xs
    return y.reshape(bsz, s, BRANCH_WIDTH)


def setup_inputs(seed: int = 0) -> dict:
    key = jax.random.key(seed)
    ks = jax.random.split(key, 24)
    f32 = jnp.float32

    def nrm(k, shape, scale):
        return scale * jax.random.normal(k, shape, f32)

    x = jax.random.normal(ks[0], (BATCH, SEQ, D_MODEL), f32)
    norm_w = 1.0 + nrm(ks[1], (DEPTH, D_MODEL), 0.02)
    w_in = nrm(ks[2], (DEPTH, D_MODEL, IN_COLS), D_MODEL ** -0.5)
    hgrn_lb_logits = nrm(ks[3], (DEPTH, A_FDIM), 0.5)
    hgrn_norm_w = 1.0 + nrm(ks[4], (DEPTH, BRANCH_WIDTH), 0.02)
    rglru_conv_w = nrm(ks[5], (DEPTH, B_CONV, BRANCH_WIDTH), B_CONV ** -0.5)
    rglru_conv_b = nrm(ks[6], (DEPTH, BRANCH_WIDTH), 0.02)
    rglru_wa = nrm(ks[7], (DEPTH, B_HEADS, B_HDIM, B_HDIM), B_HDIM ** -0.5)
    rglru_ba = nrm(ks[8], (DEPTH, B_HEADS, B_HDIM), 0.02)
    rglru_wx = nrm(ks[9], (DEPTH, B_HEADS, B_HDIM, B_HDIM), B_HDIM ** -0.5)
    rglru_bx = nrm(ks[10], (DEPTH, B_HEADS, B_HDIM), 0.02)
    u = jax.random.uniform(ks[11], (DEPTH, BRANCH_WIDTH), f32, 0.9, 0.999)
    p = u ** (1.0 / B_C)
    rglru_lambda = jnp.log(p) - jnp.log1p(-p)
    ssd_conv_w = nrm(ks[12], (DEPTH, C_CONV, C_CONV_DIM), C_CONV ** -0.5)
    ssd_conv_b = nrm(ks[13], (DEPTH, C_CONV_DIM), 0.02)
    dt0 = jnp.exp(jax.random.uniform(ks[14], (DEPTH, C_HEADS), f32, float(np.log(1e-3)), float(np.log(1e-1))))
    ssd_dt_bias = dt0 + jnp.log(-jnp.expm1(-dt0))
    ssd_a_log = jnp.log(jax.random.uniform(ks[15], (DEPTH, C_HEADS), f32, 1.0, 16.0))
    ssd_d = 1.0 + nrm(ks[16], (DEPTH, C_HEADS), 0.1)
    ssd_norm_w = 1.0 + nrm(ks[17], (DEPTH, BRANCH_WIDTH), 0.02)
    w_out = nrm(ks[18], (DEPTH, N_BRANCH * BRANCH_WIDTH, D_MODEL), (N_BRANCH * BRANCH_WIDTH) ** -0.5)
    final_norm_w = 1.0 + nrm(ks[19], (D_MODEL,), 0.02)
    return {'x': x, 'norm_w': norm_w, 'w_in': w_in, 'hgrn_lb_logits': hgrn_lb_logits,
            'hgrn_norm_w': hgrn_norm_w, 'rglru_conv_w': rglru_conv_w, 'rglru_conv_b': rglru_conv_b,
            'rglru_wa': rglru_wa, 'rglru_ba': rglru_ba, 'rglru_wx': rglru_wx, 'rglru_bx': rglru_bx,
            'rglru_lambda': rglru_lambda, 'ssd_conv_w': ssd_conv_w, 'ssd_conv_b': ssd_conv_b,
            'ssd_dt_bias': ssd_dt_bias, 'ssd_a_log': ssd_a_log, 'ssd_d': ssd_d, 'ssd_norm_w': ssd_norm_w,
            'w_out': w_out, 'final_norm_w': final_norm_w}


def reference(x, norm_w, w_in, hgrn_lb_logits, hgrn_norm_w, rglru_conv_w, rglru_conv_b, rglru_wa, rglru_ba,
              rglru_wx, rglru_bx, rglru_lambda, ssd_conv_w, ssd_conv_b, ssd_dt_bias, ssd_a_log, ssd_d,
              ssd_norm_w, w_out, final_norm_w):
    split_idx = np.cumsum(SPLITS)[:-1].tolist()
    sm = jax.nn.softmax(hgrn_lb_logits.astype(jnp.float32), axis=0)
    cs = jnp.cumsum(sm, axis=0)
    lower_bounds = cs - cs[0:1]
    bsz, s, _ = x.shape
    for l in range(DEPTH):
        h = rmsnorm(x, norm_w[l])
        proj = h @ w_in[l]
        a_q, a_f, a_i, a_z, b_x, b_z, c_z, c_xbc, c_dt, gate_logits = jnp.split(proj, split_idx, axis=-1)
        o_a = hgrn2_mixer(a_q, a_f, a_i, lower_bounds[l])
        o_a = rmsnorm(o_a, hgrn_norm_w[l].reshape(A_HEADS, A_VAL)).reshape(bsz, s, BRANCH_WIDTH)
        o_a = o_a * jax.nn.silu(a_z)
        o_b = rglru_mixer(b_x, rglru_conv_w[l], rglru_conv_b[l], rglru_wa[l], rglru_ba[l], rglru_wx[l],
                          rglru_bx[l], rglru_lambda[l]) * jax.nn.silu(b_z)
        y_c = ssd_mixer(c_xbc, c_dt, ssd_conv_w[l], ssd_conv_b[l], ssd_dt_bias[l], ssd_a_log[l], ssd_d[l])
        o_c = rmsnorm((y_c * jax.nn.silu(c_z)).reshape(bsz, s, C_GROUPS, BRANCH_WIDTH // C_GROUPS),
                      ssd_norm_w[l].reshape(C_GROUPS, BRANCH_WIDTH // C_GROUPS)).reshape(bsz, s, BRANCH_WIDTH)
        branches = jnp.stack([o_a, o_b, o_c], axis=2).astype(x.dtype)
        branch_out = jnp.einsum('bsnc,ncd->bsnd', branches,
                                w_out[l].reshape(N_BRANCH, BRANCH_WIDTH, D_MODEL))
        gates = jax.nn.sigmoid(gate_logits.astype(jnp.float32).reshape(bsz, s, N_BRANCH, D_MODEL))
        x = x + jnp.sum(gates * branch_out, axis=2).astype(x.dtype)
    return rmsnorm(x, final_norm_w)
```

```python
import functools

import jax
import jax.numpy as jnp
from jax import lax
from jax.experimental import pallas as pl
from jax.experimental.pallas import tpu as pltpu

F32 = jnp.float32
BF16 = jnp.bfloat16

D_MODEL = 2048
BATCH = 2
SEQ = 4096
DEPTH = 4
TOKENS = BATCH * SEQ
EPS = 1e-6
WIDTH = 1024
N_BRANCH = 3
LANES = 128
SUBLANES = 8
A_HEADS = 8
A_CHUNK = 64
A_SUB = 16
B_HEADS = 8
B_CONV = 4
B_C = 8.0
C_HDIM = 64
C_HEADS = 16
C_GROUPS = 2
C_STATE = 128
C_CONV = 4
C_CHUNK = 128
C_GROUP_W = WIDTH // C_GROUPS
C_GROUP_HEADS = C_HEADS // C_GROUPS
MAIN_COLS = 8 * WIDTH + 2 * C_GROUPS * C_STATE
BLK_AQ, BLK_AF, BLK_AI, BLK_AZ = 0, 8, 16, 24
BLK_BX, BLK_BZ = 32, 40
BLK_CZ, BLK_CX, BLK_CB, BLK_CC = 48, 56, 64, 66
DT_COL0 = MAIN_COLS
GATE_COL0 = MAIN_COLS + C_HEADS
GATE_COLS = N_BRANCH * D_MODEL
CONV_HALO = SUBLANES
VMEM_LIMIT = 56 * 1024 * 1024

_NT = (((1,), (1,)), ((), ()))
_TN = (((0,), (0,)), ((), ()))


def _cparams(n_axes):
    return pltpu.CompilerParams(dimension_semantics=("arbitrary",) * n_axes,
                                vmem_limit_bytes=VMEM_LIMIT)


def _sigmoid(x):
    return 1.0 / (1.0 + jnp.exp(-x))


def _softplus(x):
    return jnp.maximum(x, 0.0) + jnp.log1p(jnp.exp(-jnp.abs(x)))


def _split3(x):
    x1 = x.astype(BF16)
    r1 = x - x1.astype(F32)
    x2 = r1.astype(BF16)
    x3 = (r1 - x2.astype(F32)).astype(BF16)
    return x1, x2, x3


def _dot_exact_lhs(m01, x):
    x1, x2, x3 = _split3(x)
    d = lambda a: jnp.dot(m01, a, preferred_element_type=F32)
    return d(x1) + d(x2) + d(x3)


def _dot_exact_rhs(x, m01):
    x1, x2, x3 = _split3(x)
    d = lambda a: jnp.dot(a, m01, preferred_element_type=F32)
    return d(x1) + d(x2) + d(x3)


def _tril(n):
    r = lax.broadcasted_iota(jnp.int32, (n, n), 0)
    c = lax.broadcasted_iota(jnp.int32, (n, n), 1)
    return r >= c


def _rms_kernel(x_ref, w_ref, o_ref):
    x = x_ref[...]
    ms = jnp.mean(x * x, axis=-1, keepdims=True)
    o_ref[...] = (x * lax.rsqrt(ms + EPS) * w_ref[...]).astype(o_ref.dtype)


def _rmsnorm(x, w, out_dtype, tm=512):
    t, d = x.shape
    return pl.pallas_call(
        _rms_kernel,
        grid=(t // tm,),
        in_specs=[pl.BlockSpec((tm, d), lambda i: (i, 0)),
                  pl.BlockSpec((1, d), lambda i: (0, 0))],
        out_specs=pl.BlockSpec((tm, d), lambda i: (i, 0)),
        out_shape=jax.ShapeDtypeStruct((t, d), out_dtype),
        compiler_params=_cparams(1),
        name="rmsnorm",
    )(x, w)


def _proj_kernel(h_ref, w_ref, o_ref):
    acc = jnp.dot(h_ref[...], w_ref[...], preferred_element_type=F32)
    for c in range(o_ref.shape[0]):
        o_ref[c] = acc[:, c * LANES:(c + 1) * LANES].astype(o_ref.dtype)


def _project(h, w_all, layer, out_dtype, tm, tn, name):
    t, d = h.shape
    n = w_all.shape[-1]
    return pl.pallas_call(
        _proj_kernel,
        grid=(t // tm, n // tn),
        in_specs=[pl.BlockSpec((tm, d), lambda i, j: (i, 0)),
                  pl.BlockSpec((None, d, tn), lambda i, j: (layer, 0, j))],
        out_specs=pl.BlockSpec((tn // LANES, tm, LANES), lambda i, j: (j, i, 0)),
        out_shape=jax.ShapeDtypeStruct((n // LANES, t, LANES), out_dtype),
        compiler_params=_cparams(2),
        name=name,
    )(h, w_all)


def _hgrn_kernel(layer, q_ref, f_ref, i_ref, z_ref, lbl_ref, nw_ref, o_ref, st_ref):
    tt = q_ref.shape[1]

    @pl.when(pl.program_id(2) == 0)
    def _():
        st_ref[...] = jnp.zeros_like(st_ref)

    lbl = lbl_ref[0]
    e = jnp.exp(lbl - jnp.max(lbl, axis=0, keepdims=True))
    lb = jnp.zeros((1, LANES), F32)
    for j in range(1, layer + 1):
        lb = lb + e[j:j + 1, :]
    lb = lb / jnp.sum(e, axis=0, keepdims=True)
    nw = nw_ref[0]

    tril_c = _tril(A_CHUNK).astype(BF16)
    n_sub = A_CHUNK // A_SUB

    def chunk(c, carry):
        r0 = pl.multiple_of(c * A_CHUNK, A_CHUNK)
        rows = pl.ds(r0, A_CHUNK)
        fx = f_ref[0, rows, :].astype(F32)
        q = q_ref[0, rows, :].astype(F32)
        v = i_ref[0, rows, :]
        z = z_ref[0, rows, :].astype(F32)
        f = lb + (1.0 - lb) * _sigmoid(fx)
        g = jnp.maximum(jnp.log(f), -128.0)
        kk = 1.0 - f
        b = _dot_exact_lhs(tril_c, g)
        st = st_ref[...]
        outs = []
        for i in range(n_sub):
            lo, hi = A_SUB * i, A_SUB * (i + 1)
            ri = b[lo - 1:lo, :] if i > 0 else jnp.zeros((1, LANES), F32)
            qi = (q[lo:hi] * jnp.exp(b[lo:hi] - ri)).astype(BF16)
            ki = (kk[:hi] * jnp.exp(jnp.minimum(ri - b[:hi], 80.0))).astype(BF16)
            a_i = lax.dot_general(qi, ki, _NT, preferred_element_type=F32)
            sub_r = lax.broadcasted_iota(jnp.int32, (A_SUB, hi), 0)
            sub_c = lax.broadcasted_iota(jnp.int32, (A_SUB, hi), 1)
            a_i = jnp.where(sub_c <= sub_r + lo, a_i, 0.0)
            outs.append(jnp.dot(a_i.astype(BF16), v[:hi], preferred_element_type=F32))
        o = jnp.concatenate(outs, axis=0)
        qs = (q * jnp.exp(b)).astype(BF16)
        o = o + lax.dot_general(qs, st.astype(BF16), _NT, preferred_element_type=F32)
        bl = b[A_CHUNK - 1:A_CHUNK, :]
        kd = (kk * jnp.exp(bl - b)).astype(BF16)
        st_ref[...] = st * jnp.exp(bl) + lax.dot_general(v, kd, _TN, preferred_element_type=F32)
        ms = jnp.mean(o * o, axis=-1, keepdims=True)
        y = o * lax.rsqrt(ms + EPS) * nw
        o_ref[0, rows, :] = (y * (z * _sigmoid(z))).astype(o_ref.dtype)
        return carry

    lax.fori_loop(0, tt // A_CHUNK, chunk, 0)


def _hgrn2(proj, lbl, nw, layer, tt=1024):
    nt = SEQ // tt
    blk = lambda off: pl.BlockSpec((1, tt, LANES), lambda b, h, t: (off + h, b * nt + t, 0))
    return pl.pallas_call(
        functools.partial(_hgrn_kernel, layer),
        grid=(BATCH, A_HEADS, nt),
        in_specs=[blk(BLK_AQ), blk(BLK_AF), blk(BLK_AI), blk(BLK_AZ),
                  pl.BlockSpec((1, DEPTH, LANES), lambda b, h, t: (h, 0, 0)),
                  pl.BlockSpec((None, 1, 1, LANES), lambda b, h, t: (layer, h, 0, 0))],
        out_specs=pl.BlockSpec((1, tt, LANES), lambda b, h, t: (h, b * nt + t, 0)),
        out_shape=jax.ShapeDtypeStruct((A_HEADS, TOKENS, LANES), BF16),
        scratch_shapes=[pltpu.VMEM((LANES, LANES), F32)],
        compiler_params=_cparams(3),
        name="hgrn2",
    )(proj, proj, proj, proj, lbl, nw)


def _causal_conv(xbuf, x, cw, cb, first):
    tt = x.shape[0]

    @pl.when(first)
    def _():
        xbuf[0:CONV_HALO, :] = jnp.zeros((CONV_HALO, LANES), F32)

    xbuf[CONV_HALO:CONV_HALO + tt, :] = x
    y = cb
    for k in range(B_CONV):
        y = y + cw[k:k + 1, :] * xbuf[pl.ds(CONV_HALO - (B_CONV - 1) + k, tt), :]
    xbuf[0:CONV_HALO, :] = x[tt - CONV_HALO:tt, :]
    return y


def _rglru_kernel(x_ref, z_ref, cw_ref, cb_ref, wa_ref, ba_ref, wx_ref, bx_ref, lam_ref, o_ref,
                  xbuf, a_s, u_s, hcar):
    tt = x_ref.shape[1]
    first = pl.program_id(2) == 0

    @pl.when(first)
    def _():
        hcar[...] = jnp.zeros_like(hcar)

    xc = _causal_conv(xbuf, x_ref[0].astype(F32), cw_ref[...], cb_ref[...], first)
    xcb = xc.astype(BF16)
    r = _sigmoid(jnp.dot(xcb, wa_ref[...], preferred_element_type=F32) + ba_ref[...])
    ig = _sigmoid(jnp.dot(xcb, wx_ref[...], preferred_element_type=F32) + bx_ref[...])
    log_a = (-B_C) * r * _softplus(-lam_ref[...])
    a = jnp.exp(log_a)
    u = jnp.sqrt(1.0 - jnp.exp(2.0 * log_a)) * (ig * xc)
    rowm = lax.broadcasted_iota(jnp.int32, (tt, LANES), 0) & (SUBLANES - 1)
    for s in (1, 2, 4):
        a_sh = pltpu.roll(a, s, 0)
        u_sh = pltpu.roll(u, s, 0)
        m = rowm >= s
        u = jnp.where(m, a * u_sh + u, u)
        a = jnp.where(m, a * a_sh, a)
    a_s[...] = a
    u_s[...] = u

    def grp(g, h):
        rows = pl.ds(pl.multiple_of(g * SUBLANES, SUBLANES), SUBLANES)
        hg = a_s[rows, :] * h + u_s[rows, :]
        u_s[rows, :] = hg
        return jnp.broadcast_to(hg[SUBLANES - 1:SUBLANES, :], (SUBLANES, LANES))

    hcar[...] = lax.fori_loop(0, tt // SUBLANES, grp, hcar[...], unroll=8)
    z = z_ref[0].astype(F32)
    o_ref[0] = (u_s[...] * (z * _sigmoid(z))).astype(o_ref.dtype)


def _rglru(proj, cw, cb, wa, ba, wx, bx, lam, layer, tt=512):
    nt = SEQ // tt
    blk = lambda off: pl.BlockSpec((1, tt, LANES), lambda b, h, t: (off + h, b * nt + t, 0))
    par = lambda r, c: pl.BlockSpec((None, None, r, c), lambda b, h, t: (layer, h, 0, 0))
    return pl.pallas_call(
        _rglru_kernel,
        grid=(BATCH, B_HEADS, nt),
        in_specs=[blk(BLK_BX), blk(BLK_BZ), par(B_CONV, LANES), par(1, LANES),
                  par(LANES, LANES), par(1, LANES), par(LANES, LANES), par(1, LANES), par(1, LANES)],
        out_specs=pl.BlockSpec((1, tt, LANES), lambda b, h, t: (h, b * nt + t, 0)),
        out_shape=jax.ShapeDtypeStruct((B_HEADS, TOKENS, LANES), BF16),
        scratch_shapes=[pltpu.VMEM((tt + CONV_HALO, LANES), F32),
                        pltpu.VMEM((tt, LANES), F32),
                        pltpu.VMEM((tt, LANES), F32),
                        pltpu.VMEM((SUBLANES, LANES), F32)],
        compiler_params=_cparams(3),
        name="rglru",
    )(proj, proj, cw, cb, wa, ba, wx, bx, lam)


def _ssd_kernel(z_ref, x_ref, bm_ref, cm_ref, dt_ref, cwx_ref, cwb_ref, cwc_ref, cbx_ref, cbb_ref, cbc_ref,
                dtb_ref, alog_ref, dsk_ref, nw_ref, o_ref, xbuf, st_ref):
    tt = x_ref.shape[1]
    nxb = C_GROUP_W // LANES
    first = pl.program_id(2) == 0

    @pl.when(first)
    def _():
        st_ref[...] = jnp.zeros_like(st_ref)

    silu = lambda t: t * _sigmoid(t)
    xs_blocks = [silu(_causal_conv(xbuf.at[p], x_ref[p].astype(F32), cwx_ref[p], cbx_ref[p], first))
                 for p in range(nxb)]
    bm = silu(_causal_conv(xbuf.at[nxb], bm_ref[0].astype(F32), cwb_ref[0], cbb_ref[0], first))
    cm = silu(_causal_conv(xbuf.at[nxb + 1], cm_ref[0].astype(F32), cwc_ref[0], cbc_ref[0], first))

    dt = _softplus(dt_ref[0] + dtb_ref[0])
    da = dt * (-jnp.exp(alog_ref[0]))
    causal = _tril(tt)
    cum = _dot_exact_lhs(causal.astype(BF16), da)
    cum_t = cum.T
    dt_t = dt.T
    cb = lax.dot_general(cm.astype(BF16), bm.astype(BF16), _NT, preferred_element_type=F32)
    lane = lax.broadcasted_iota(jnp.int32, (tt, LANES), 1)
    y_blocks = []
    for p in range(nxb):
        xb = xs_blocks[p]
        acc = None
        for half in range(2):
            j = 2 * p + half
            seg = cum[:, j:j + 1] - cum_t[j:j + 1, :]
            lmat = jnp.where(causal, jnp.exp(jnp.minimum(seg, 0.0)), 0.0)
            m = (cb * lmat * dt_t[j:j + 1, :]).astype(BF16)
            in_head = (lane >= C_HDIM) if half else (lane < C_HDIM)
            rhs = jnp.where(in_head, xb, 0.0).astype(BF16)
            part = jnp.dot(m, rhs, preferred_element_type=F32)
            acc = part if acc is None else acc + part
        y_blocks.append(acc)
    xs = jnp.concatenate(xs_blocks, axis=1)
    y = jnp.concatenate(y_blocks, axis=1)

    er = lax.broadcasted_iota(jnp.int32, (LANES, C_GROUP_W), 0)
    ec = lax.broadcasted_iota(jnp.int32, (LANES, C_GROUP_W), 1)
    spread = ((ec >= er * C_HDIM) & (ec < (er + 1) * C_HDIM)).astype(BF16)
    st = st_ref[...]
    y = y + jnp.dot(cm.astype(BF16), st.astype(BF16), preferred_element_type=F32) * _dot_exact_rhs(jnp.exp(cum), spread)
    y = y + jnp.concatenate([dsk_ref[p] for p in range(nxb)], axis=1) * xs
    cl = cum[tt - 1:tt, :]
    w = dt * jnp.exp(cl - cum)
    xw = (xs * _dot_exact_rhs(w, spread)).astype(BF16)
    dec = _dot_exact_rhs(jnp.broadcast_to(jnp.exp(cl), (SUBLANES, LANES)), spread)[0:1, :]
    st_ref[...] = st * dec + lax.dot_general(bm.astype(BF16), xw, _TN, preferred_element_type=F32)
    zf = jnp.concatenate([z_ref[p].astype(F32) for p in range(nxb)], axis=1)
    yg = y * silu(zf)
    ms = jnp.mean(yg * yg, axis=-1, keepdims=True)
    out = yg * lax.rsqrt(ms + EPS) * jnp.concatenate([nw_ref[p] for p in range(nxb)], axis=1)
    for p in range(nxb):
        o_ref[p] = out[:, p * LANES:(p + 1) * LANES].astype(o_ref.dtype)


def _ssd(proj, dtp, cw, cb, dtb, alog, dsk, nw, layer):
    tt = C_CHUNK
    nt = SEQ // tt
    nxb = C_GROUP_W // LANES
    big = lambda off: pl.BlockSpec((nxb, tt, LANES), lambda b, g, t: (off // nxb + g, b * nt + t, 0))
    one = lambda off: pl.BlockSpec((1, tt, LANES), lambda b, g, t: (off + g, b * nt + t, 0))
    cw_big = pl.BlockSpec((None, nxb, C_CONV, LANES), lambda b, g, t: (layer, g, 0, 0))
    cw_one = lambda off: pl.BlockSpec((None, 1, C_CONV, LANES), lambda b, g, t: (layer, off + g, 0, 0))
    cb_big = pl.BlockSpec((None, nxb, 1, LANES), lambda b, g, t: (layer, g, 0, 0))
    cb_one = lambda off: pl.BlockSpec((None, 1, 1, LANES), lambda b, g, t: (layer, off + g, 0, 0))
    grp = pl.BlockSpec((None, 1, 1, LANES), lambda b, g, t: (layer, g, 0, 0))
    return pl.pallas_call(
        _ssd_kernel,
        grid=(BATCH, C_GROUPS, nt),
        in_specs=[big(BLK_CZ), big(BLK_CX), one(BLK_CB), one(BLK_CC),
                  pl.BlockSpec((1, tt, LANES), lambda b, g, t: (g, b * nt + t, 0)),
                  cw_big, cw_one(2 * nxb), cw_one(2 * nxb + C_GROUPS),
                  cb_big, cb_one(2 * nxb), cb_one(2 * nxb + C_GROUPS),
                  grp, grp, cb_big, cb_big],
        out_specs=pl.BlockSpec((nxb, tt, LANES), lambda b, g, t: (g, b * nt + t, 0)),
        out_shape=jax.ShapeDtypeStruct((WIDTH // LANES, TOKENS, LANES), BF16),
        scratch_shapes=[pltpu.VMEM((nxb + 2, tt + CONV_HALO, LANES), F32),
                        pltpu.VMEM((C_STATE, C_GROUP_W), F32)],
        compiler_params=_cparams(3),
        name="ssd",
    )(proj, proj, proj, proj, dtp, cw, cw, cw, cb, cb, cb, dtb, alog, dsk, nw)


def _merge_kernel(final, oa_ref, ob_ref, oc_ref, g_ref, x_ref, w_ref, nw_ref, *out_refs):
    nb = WIDTH // LANES
    ngb = D_MODEL // LANES
    acc = x_ref[...]
    for br, o_ref in enumerate((oa_ref, ob_ref, oc_ref)):
        o = jnp.concatenate([o_ref[hh] for hh in range(nb)], axis=1)
        bo = jnp.dot(o, w_ref[br], preferred_element_type=F32)
        gl = jnp.concatenate([g_ref[br * ngb + cbk] for cbk in range(ngb)], axis=1).astype(F32)
        acc = acc + _sigmoid(gl) * bo
    ms = jnp.mean(acc * acc, axis=-1, keepdims=True)
    hn = acc * lax.rsqrt(ms + EPS) * nw_ref[...]
    if final:
        out_refs[0][...] = hn.astype(out_refs[0].dtype)
    else:
        out_refs[0][...] = acc
        out_refs[1][...] = hn.astype(out_refs[1].dtype)


def _merge(oa, ob, oc, gates, x, w_out, nw, layer, final, tm=256):
    nb = WIDTH // LANES
    ngb = GATE_COLS // LANES
    obs = pl.BlockSpec((nb, tm, LANES), lambda i: (0, i, 0))
    row = pl.BlockSpec((tm, D_MODEL), lambda i: (i, 0))
    if final:
        out_specs = row
        out_shape = jax.ShapeDtypeStruct((TOKENS, D_MODEL), F32)
    else:
        out_specs = [row, row]
        out_shape = [jax.ShapeDtypeStruct((TOKENS, D_MODEL), F32),
                     jax.ShapeDtypeStruct((TOKENS, D_MODEL), BF16)]
    return pl.pallas_call(
        functools.partial(_merge_kernel, final),
        grid=(TOKENS // tm,),
        in_specs=[obs, obs, obs,
                  pl.BlockSpec((ngb, tm, LANES), lambda i: (0, i, 0)),
                  row,
                  pl.BlockSpec((None, N_BRANCH, WIDTH, D_MODEL), lambda i: (layer, 0, 0, 0)),
                  pl.BlockSpec((1, D_MODEL), lambda i: (0, 0))],
        out_specs=out_specs,
        out_shape=out_shape,
        compiler_params=_cparams(1),
        name="merge",
    )(oa, ob, oc, gates, x, w_out, nw)


def _blocks(p, rows):
    d, r, n = p.shape
    return p.reshape(d, r, n // LANES, LANES).transpose(0, 2, 1, 3)


def kernel(x, norm_w, w_in, hgrn_lb_logits, hgrn_norm_w, rglru_conv_w, rglru_conv_b, rglru_wa, rglru_ba,
           rglru_wx, rglru_bx, rglru_lambda, ssd_conv_w, ssd_conv_b, ssd_dt_bias, ssd_a_log, ssd_d,
           ssd_norm_w, w_out, final_norm_w):
    w_main = w_in[:, :, :MAIN_COLS].astype(BF16)
    w_gate = w_in[:, :, GATE_COL0:].astype(BF16)
    w_dt = w_in[:, :, DT_COL0:GATE_COL0].reshape(DEPTH, D_MODEL, C_GROUPS, C_GROUP_HEADS)
    w_dt = jnp.pad(w_dt, ((0, 0), (0, 0), (0, 0), (0, LANES - C_GROUP_HEADS)))
    w_dt = w_dt.reshape(DEPTH, D_MODEL, C_GROUPS * LANES).astype(BF16)
    w_o = w_out.reshape(DEPTH, N_BRANCH, WIDTH, D_MODEL).astype(BF16)
    lbl = hgrn_lb_logits.reshape(DEPTH, A_HEADS, LANES).transpose(1, 0, 2)
    a_nw = _blocks(hgrn_norm_w[:, None, :], 1)
    b_cw = _blocks(rglru_conv_w, B_CONV)
    b_cb = _blocks(rglru_conv_b[:, None, :], 1)
    b_wa = rglru_wa.astype(BF16)
    b_wx = rglru_wx.astype(BF16)
    b_ba = rglru_ba[:, :, None, :]
    b_bx = rglru_bx[:, :, None, :]
    b_lam = _blocks(rglru_lambda[:, None, :], 1)
    c_cw = _blocks(ssd_conv_w, C_CONV)
    c_cb = _blocks(ssd_conv_b[:, None, :], 1)
    pad_heads = lambda p: jnp.pad(p.reshape(DEPTH, C_GROUPS, 1, C_GROUP_HEADS),
                                  ((0, 0), (0, 0), (0, 0), (0, LANES - C_GROUP_HEADS)))
    c_dtb = pad_heads(ssd_dt_bias)
    c_alog = pad_heads(ssd_a_log)
    c_dsk = _blocks(jnp.repeat(ssd_d, C_HDIM, axis=1)[:, None, :], 1)
    c_nw = _blocks(ssd_norm_w[:, None, :], 1)

    xf = x.reshape(TOKENS, D_MODEL)
    h = _rmsnorm(xf, norm_w[0:1], BF16)
    for l in range(DEPTH):
        proj = _project(h, w_main, l, BF16, tm=2048, tn=512, name="proj_main")
        gates = _project(h, w_gate, l, BF16, tm=2048, tn=512, name="proj_gate")
        dtp = _project(h, w_dt, l, F32, tm=2048, tn=256, name="proj_dt")
        oa = _hgrn2(proj, lbl, a_nw, l)
        ob = _rglru(proj, b_cw, b_cb, b_wa, b_ba, b_wx, b_bx, b_lam, l)
        oc = _ssd(proj, dtp, c_cw, c_cb, c_dtb, c_alog, c_dsk, c_nw, l)
        if l + 1 < DEPTH:
            xf, h = _merge(oa, ob, oc, gates, xf, w_o, norm_w[l + 1:l + 2], l, final=False)
        else:
            out = _merge(oa, ob, oc, gates, xf, w_o, final_norm_w[None, :], l, final=True)
    return out.reshape(BATCH, SEQ, D_MODEL)
```

```python
import functools

import jax
import jax.numpy as jnp
from jax import lax
from jax.experimental import pallas as pl
from jax.experimental.pallas import tpu as pltpu

F32 = jnp.float32
BF16 = jnp.bfloat16

D_MODEL = 2048
BATCH = 2
SEQ = 4096
DEPTH = 4
TOKENS = BATCH * SEQ
EPS = 1e-6
WIDTH = 1024
N_BRANCH = 3
LANES = 128
SUBLANES = 8
A_HEADS = 8
A_CHUNK = 64
A_MAX_DECAY = 80.0
B_HEADS = 8
B_CONV = 4
B_C = 8.0
C_HDIM = 64
C_HEADS = 16
C_GROUPS = 2
C_STATE = 128
C_CONV = 4
C_CHUNK = 128
C_GROUP_W = WIDTH // C_GROUPS
C_GROUP_HEADS = C_HEADS // C_GROUPS
MAIN_COLS = 8 * WIDTH + 2 * C_GROUPS * C_STATE
BLK_AQ, BLK_AF, BLK_AI, BLK_AZ = 0, 8, 16, 24
BLK_BX, BLK_BZ = 32, 40
BLK_CZ, BLK_CX, BLK_CB, BLK_CC = 48, 56, 64, 66
DT_COL0 = MAIN_COLS
GATE_COL0 = MAIN_COLS + C_HEADS
GATE_COLS = N_BRANCH * D_MODEL
CONV_HALO = SUBLANES
VMEM_LIMIT = 56 * 1024 * 1024

_NT = (((1,), (1,)), ((), ()))
_TN = (((0,), (0,)), ((), ()))


def _cparams(n_axes):
    return pltpu.CompilerParams(dimension_semantics=("arbitrary",) * n_axes,
                                vmem_limit_bytes=VMEM_LIMIT)


def _sigmoid(x):
    return 1.0 / (1.0 + jnp.exp(-x))


def _softplus(x):
    return jnp.maximum(x, 0.0) + jnp.log1p(jnp.exp(-jnp.abs(x)))


def _split3(x):
    x1 = x.astype(BF16)
    r1 = x - x1.astype(F32)
    x2 = r1.astype(BF16)
    x3 = (r1 - x2.astype(F32)).astype(BF16)
    return x1, x2, x3


def _dot_exact_lhs(m01, x):
    x1, x2, x3 = _split3(x)
    d = lambda a: jnp.dot(m01, a, preferred_element_type=F32)
    return d(x1) + d(x2) + d(x3)


def _dot_exact_rhs(x, m01):
    x1, x2, x3 = _split3(x)
    d = lambda a: jnp.dot(a, m01, preferred_element_type=F32)
    return d(x1) + d(x2) + d(x3)


def _tril(n):
    r = lax.broadcasted_iota(jnp.int32, (n, n), 0)
    c = lax.broadcasted_iota(jnp.int32, (n, n), 1)
    return r >= c


def _rms_kernel(x_ref, w_ref, o_ref):
    x = x_ref[...]
    ms = jnp.mean(x * x, axis=-1, keepdims=True)
    o_ref[...] = (x * lax.rsqrt(ms + EPS) * w_ref[...]).astype(o_ref.dtype)


def _rmsnorm(x, w, out_dtype, tm=512):
    t, d = x.shape
    return pl.pallas_call(
        _rms_kernel,
        grid=(t // tm,),
        in_specs=[pl.BlockSpec((tm, d), lambda i: (i, 0)),
                  pl.BlockSpec((1, d), lambda i: (0, 0))],
        out_specs=pl.BlockSpec((tm, d), lambda i: (i, 0)),
        out_shape=jax.ShapeDtypeStruct((t, d), out_dtype),
        compiler_params=_cparams(1),
        name="rmsnorm",
    )(x, w)


def _proj_kernel(h_ref, w_ref, o_ref):
    acc = jnp.dot(h_ref[...], w_ref[...], preferred_element_type=F32)
    for c in range(o_ref.shape[0]):
        o_ref[c] = acc[:, c * LANES:(c + 1) * LANES].astype(o_ref.dtype)


def _project(h, w_all, layer, out_dtype, tm, tn, name):
    t, d = h.shape
    n = w_all.shape[-1]
    return pl.pallas_call(
        _proj_kernel,
        grid=(t // tm, n // tn),
        in_specs=[pl.BlockSpec((tm, d), lambda i, j: (i, 0)),
                  pl.BlockSpec((None, d, tn), lambda i, j: (layer, 0, j))],
        out_specs=pl.BlockSpec((tn // LANES, tm, LANES), lambda i, j: (j, i, 0)),
        out_shape=jax.ShapeDtypeStruct((n // LANES, t, LANES), out_dtype),
        compiler_params=_cparams(2),
        name=name,
    )(h, w_all)


def _hgrn_kernel(layer, q_ref, f_ref, i_ref, z_ref, lbl_ref, nw_ref, o_ref, st_ref, b_ref, k_ref, a_ref):
    tt = q_ref.shape[1]

    @pl.when(pl.program_id(2) == 0)
    def _():
        st_ref[...] = jnp.zeros_like(st_ref)

    lbl = lbl_ref[0]
    e = jnp.exp(lbl - jnp.max(lbl, axis=0, keepdims=True))
    lb = jnp.zeros((1, LANES), F32)
    for j in range(1, layer + 1):
        lb = lb + e[j:j + 1, :]
    lb = lb / jnp.sum(e, axis=0, keepdims=True)
    nw = nw_ref[0]

    nc = tt // A_CHUNK
    chunks = [slice(c * A_CHUNK, (c + 1) * A_CHUNK) for c in range(nc)]
    fx = f_ref[0].astype(F32)
    f = lb + (1.0 - lb) * _sigmoid(fx)
    g = jnp.maximum(jnp.log(f), -128.0)
    kk = 1.0 - f
    rc = lax.broadcasted_iota(jnp.int32, (tt, LANES), 0) & (A_CHUNK - 1)
    b = g
    s = 1
    while s < A_CHUNK:
        b = b + jnp.where(rc >= s, pltpu.roll(b, s, 0), 0.0)
        s *= 2
    q = q_ref[0].astype(F32)
    qs = (q * jnp.exp(b)).astype(BF16)
    b_ref[...] = b
    k_ref[...] = kk
    causal = _tril(A_CHUNK)
    small_decay = jnp.min(b) >= -A_MAX_DECAY

    @pl.when(small_decay)
    def _():
        kt = (kk * jnp.exp(-b)).astype(BF16)
        for rows in chunks:
            a = lax.dot_general(qs[rows], kt[rows], _NT, preferred_element_type=F32)
            a_ref[rows, :] = jnp.where(causal, a, 0.0)

    @pl.when(jnp.logical_not(small_decay))
    def _():
        coli = lax.broadcasted_iota(jnp.int32, (A_CHUNK, A_CHUNK), 1)

        def chunk_body(c, carry):
            r0 = pl.multiple_of(c * A_CHUNK, A_CHUNK)
            rows = pl.ds(r0, A_CHUNK)
            bc = b_ref[rows, :]
            qc = q_ref[0, rows, :].astype(F32)

            def col_body(j, acc):
                bj = b_ref[pl.ds(r0 + j, 1), :]
                kj = k_ref[pl.ds(r0 + j, 1), :]
                p = qc * jnp.exp(jnp.minimum(bc - bj, 0.0)) * kj
                return jnp.where(coli == j, jnp.sum(p, axis=-1, keepdims=True), acc)

            acc = lax.fori_loop(0, A_CHUNK, col_body, jnp.zeros((A_CHUNK, A_CHUNK), F32))
            a_ref[rows, :] = jnp.where(causal, acc, 0.0)
            return carry

        lax.fori_loop(0, nc, chunk_body, 0)

    v = i_ref[0]
    b_last = [b[r.stop - 1:r.stop, :] for r in chunks]
    bl_full = jnp.concatenate([jnp.broadcast_to(bl, (A_CHUNK, LANES)) for bl in b_last], axis=0)
    kd = (kk * jnp.exp(bl_full - b)).astype(BF16)
    o_intra = [jnp.dot(a_ref[r, :].astype(BF16), v[r], preferred_element_type=F32) for r in chunks]
    kv = [lax.dot_general(v[r], kd[r], _TN, preferred_element_type=F32) for r in chunks]
    st = st_ref[...]
    outs = []
    for c, r in enumerate(chunks):
        outs.append(o_intra[c] + lax.dot_general(qs[r], st.astype(BF16), _NT, preferred_element_type=F32))
        st = st * jnp.exp(b_last[c]) + kv[c]
    st_ref[...] = st
    o = jnp.concatenate(outs, axis=0)
    z = z_ref[0].astype(F32)
    ms = jnp.mean(o * o, axis=-1, keepdims=True)
    y = o * lax.rsqrt(ms + EPS) * nw
    o_ref[0] = (y * (z * _sigmoid(z))).astype(o_ref.dtype)


def _hgrn2(proj, lbl, nw, layer, tt=512):
    nt = SEQ // tt
    blk = lambda off: pl.BlockSpec((1, tt, LANES), lambda b, h, t: (off + h, b * nt + t, 0))
    return pl.pallas_call(
        functools.partial(_hgrn_kernel, layer),
        grid=(BATCH, A_HEADS, nt),
        in_specs=[blk(BLK_AQ), blk(BLK_AF), blk(BLK_AI), blk(BLK_AZ),
                  pl.BlockSpec((1, DEPTH, LANES), lambda b, h, t: (h, 0, 0)),
                  pl.BlockSpec((None, 1, 1, LANES), lambda b, h, t: (layer, h, 0, 0))],
        out_specs=pl.BlockSpec((1, tt, LANES), lambda b, h, t: (h, b * nt + t, 0)),
        out_shape=jax.ShapeDtypeStruct((A_HEADS, TOKENS, LANES), BF16),
        scratch_shapes=[pltpu.VMEM((LANES, LANES), F32),
                        pltpu.VMEM((tt, LANES), F32),
                        pltpu.VMEM((tt, LANES), F32),
                        pltpu.VMEM((tt, A_CHUNK), F32)],
        compiler_params=_cparams(3),
        name="hgrn2",
    )(proj, proj, proj, proj, lbl, nw)


def _causal_conv(xbuf, x, cw, cb, first):
    tt = x.shape[0]

    @pl.when(first)
    def _():
        xbuf[0:CONV_HALO, :] = jnp.zeros((CONV_HALO, LANES), F32)

    xbuf[CONV_HALO:CONV_HALO + tt, :] = x
    y = cb
    for k in range(B_CONV):
        y = y + cw[k:k + 1, :] * xbuf[pl.ds(CONV_HALO - (B_CONV - 1) + k, tt), :]
    xbuf[0:CONV_HALO, :] = x[tt - CONV_HALO:tt, :]
    return y


def _rglru_kernel(x_ref, z_ref, cw_ref, cb_ref, wa_ref, ba_ref, wx_ref, bx_ref, lam_ref, o_ref,
                  xbuf, a_s, u_s, hcar):
    tt = x_ref.shape[1]
    first = pl.program_id(2) == 0

    @pl.when(first)
    def _():
        hcar[...] = jnp.zeros_like(hcar)

    xc = _causal_conv(xbuf, x_ref[0].astype(F32), cw_ref[...], cb_ref[...], first)
    xcb = xc.astype(BF16)
    r = _sigmoid(jnp.dot(xcb, wa_ref[...], preferred_element_type=F32) + ba_ref[...])
    ig = _sigmoid(jnp.dot(xcb, wx_ref[...], preferred_element_type=F32) + bx_ref[...])
    log_a = (-B_C) * r * _softplus(-lam_ref[...])
    a = jnp.exp(log_a)
    u = jnp.sqrt(1.0 - jnp.exp(2.0 * log_a)) * (ig * xc)
    rowm = lax.broadcasted_iota(jnp.int32, (tt, LANES), 0) & (SUBLANES - 1)
    for s in (1, 2, 4):
        a_sh = pltpu.roll(a, s, 0)
        u_sh = pltpu.roll(u, s, 0)
        m = rowm >= s
        u = jnp.where(m, a * u_sh + u, u)
        a = jnp.where(m, a * a_sh, a)
    a_s[...] = a
    u_s[...] = u

    def grp(g, h):
        rows = pl.ds(pl.multiple_of(g * SUBLANES, SUBLANES), SUBLANES)
        hg = a_s[rows, :] * h + u_s[rows, :]
        u_s[rows, :] = hg
        return jnp.broadcast_to(hg[SUBLANES - 1:SUBLANES, :], (SUBLANES, LANES))

    hcar[...] = lax.fori_loop(0, tt // SUBLANES, grp, hcar[...], unroll=8)
    z = z_ref[0].astype(F32)
    o_ref[0] = (u_s[...] * (z * _sigmoid(z))).astype(o_ref.dtype)


def _rglru(proj, cw, cb, wa, ba, wx, bx, lam, layer, tt=512):
    nt = SEQ // tt
    blk = lambda off: pl.BlockSpec((1, tt, LANES), lambda b, h, t: (off + h, b * nt + t, 0))
    par = lambda r, c: pl.BlockSpec((None, None, r, c), lambda b, h, t: (layer, h, 0, 0))
    return pl.pallas_call(
        _rglru_kernel,
        grid=(BATCH, B_HEADS, nt),
        in_specs=[blk(BLK_BX), blk(BLK_BZ), par(B_CONV, LANES), par(1, LANES),
                  par(LANES, LANES), par(1, LANES), par(LANES, LANES), par(1, LANES), par(1, LANES)],
        out_specs=pl.BlockSpec((1, tt, LANES), lambda b, h, t: (h, b * nt + t, 0)),
        out_shape=jax.ShapeDtypeStruct((B_HEADS, TOKENS, LANES), BF16),
        scratch_shapes=[pltpu.VMEM((tt + CONV_HALO, LANES), F32),
                        pltpu.VMEM((tt, LANES), F32),
                        pltpu.VMEM((tt, LANES), F32),
                        pltpu.VMEM((SUBLANES, LANES), F32)],
        compiler_params=_cparams(3),
        name="rglru",
    )(proj, proj, cw, cb, wa, ba, wx, bx, lam)


def _ssd_kernel(z_ref, x_ref, bm_ref, cm_ref, dt_ref, cwx_ref, cwb_ref, cwc_ref, cbx_ref, cbb_ref, cbc_ref,
                dtb_ref, alog_ref, dsk_ref, nw_ref, o_ref, xbuf, st_ref):
    tt = x_ref.shape[1]
    nxb = C_GROUP_W // LANES
    first = pl.program_id(2) == 0

    @pl.when(first)
    def _():
        st_ref[...] = jnp.zeros_like(st_ref)

    silu = lambda t: t * _sigmoid(t)
    xs_blocks = [silu(_causal_conv(xbuf.at[p], x_ref[p].astype(F32), cwx_ref[p], cbx_ref[p], first))
                 for p in range(nxb)]
    bm = silu(_causal_conv(xbuf.at[nxb], bm_ref[0].astype(F32), cwb_ref[0], cbb_ref[0], first))
    cm = silu(_causal_conv(xbuf.at[nxb + 1], cm_ref[0].astype(F32), cwc_ref[0], cbc_ref[0], first))

    dt = _softplus(dt_ref[0] + dtb_ref[0])
    da = dt * (-jnp.exp(alog_ref[0]))
    causal = _tril(tt)
    cum = _dot_exact_lhs(causal.astype(BF16), da)
    cum_t = cum.T
    dt_t = dt.T
    cb = lax.dot_general(cm.astype(BF16), bm.astype(BF16), _NT, preferred_element_type=F32)
    lane = lax.broadcasted_iota(jnp.int32, (tt, LANES), 1)
    y_blocks = []
    for p in range(nxb):
        xb = xs_blocks[p]
        acc = None
        for half in range(2):
            j = 2 * p + half
            seg = cum[:, j:j + 1] - cum_t[j:j + 1, :]
            lmat = jnp.where(causal, jnp.exp(jnp.minimum(seg, 0.0)), 0.0)
            m = (cb * lmat * dt_t[j:j + 1, :]).astype(BF16)
            in_head = (lane >= C_HDIM) if half else (lane < C_HDIM)
            rhs = jnp.where(in_head, xb, 0.0).astype(BF16)
            part = jnp.dot(m, rhs, preferred_element_type=F32)
            acc = part if acc is None else acc + part
        y_blocks.append(acc)
    xs = jnp.concatenate(xs_blocks, axis=1)
    y = jnp.concatenate(y_blocks, axis=1)

    er = lax.broadcasted_iota(jnp.int32, (LANES, C_GROUP_W), 0)
    ec = lax.broadcasted_iota(jnp.int32, (LANES, C_GROUP_W), 1)
    spread = ((ec >= er * C_HDIM) & (ec < (er + 1) * C_HDIM)).astype(BF16)
    st = st_ref[...]
    y = y + jnp.dot(cm.astype(BF16), st.astype(BF16), preferred_element_type=F32) * _dot_exact_rhs(jnp.exp(cum), spread)
    y = y + jnp.concatenate([dsk_ref[p] for p in range(nxb)], axis=1) * xs
    cl = cum[tt - 1:tt, :]
    w = dt * jnp.exp(cl - cum)
    xw = (xs * _dot_exact_rhs(w, spread)).astype(BF16)
    dec = _dot_exact_rhs(jnp.broadcast_to(jnp.exp(cl), (SUBLANES, LANES)), spread)[0:1, :]
    st_ref[...] = st * dec + lax.dot_general(bm.astype(BF16), xw, _TN, preferred_element_type=F32)
    zf = jnp.concatenate([z_ref[p].astype(F32) for p in range(nxb)], axis=1)
    yg = y * silu(zf)
    ms = jnp.mean(yg * yg, axis=-1, keepdims=True)
    out = yg * lax.rsqrt(ms + EPS) * jnp.concatenate([nw_ref[p] for p in range(nxb)], axis=1)
    for p in range(nxb):
        o_ref[p] = out[:, p * LANES:(p + 1) * LANES].astype(o_ref.dtype)


def _ssd(proj, dtp, cw, cb, dtb, alog, dsk, nw, layer):
    tt = C_CHUNK
    nt = SEQ // tt
    nxb = C_GROUP_W // LANES
    big = lambda off: pl.BlockSpec((nxb, tt, LANES), lambda b, g, t: (off // nxb + g, b * nt + t, 0))
    one = lambda off: pl.BlockSpec((1, tt, LANES), lambda b, g, t: (off + g, b * nt + t, 0))
    cw_big = pl.BlockSpec((None, nxb, C_CONV, LANES), lambda b, g, t: (layer, g, 0, 0))
    cw_one = lambda off: pl.BlockSpec((None, 1, C_CONV, LANES), lambda b, g, t: (layer, off + g, 0, 0))
    cb_big = pl.BlockSpec((None, nxb, 1, LANES), lambda b, g, t: (layer, g, 0, 0))
    cb_one = lambda off: pl.BlockSpec((None, 1, 1, LANES), lambda b, g, t: (layer, off + g, 0, 0))
    grp = pl.BlockSpec((None, 1, 1, LANES), lambda b, g, t: (layer, g, 0, 0))
    return pl.pallas_call(
        _ssd_kernel,
        grid=(BATCH, C_GROUPS, nt),
        in_specs=[big(BLK_CZ), big(BLK_CX), one(BLK_CB), one(BLK_CC),
                  pl.BlockSpec((1, tt, LANES), lambda b, g, t: (g, b * nt + t, 0)),
                  cw_big, cw_one(2 * nxb), cw_one(2 * nxb + C_GROUPS),
                  cb_big, cb_one(2 * nxb), cb_one(2 * nxb + C_GROUPS),
                  grp, grp, cb_big, cb_big],
        out_specs=pl.BlockSpec((nxb, tt, LANES), lambda b, g, t: (g, b * nt + t, 0)),
        out_shape=jax.ShapeDtypeStruct((WIDTH // LANES, TOKENS, LANES), BF16),
        scratch_shapes=[pltpu.VMEM((nxb + 2, tt + CONV_HALO, LANES), F32),
                        pltpu.VMEM((C_STATE, C_GROUP_W), F32)],
        compiler_params=_cparams(3),
        name="ssd",
    )(proj, proj, proj, proj, dtp, cw, cw, cw, cb, cb, cb, dtb, alog, dsk, nw)


def _merge_kernel(final, oa_ref, ob_ref, oc_ref, g_ref, x_ref, w_ref, nw_ref, *out_refs):
    nb = WIDTH // LANES
    ngb = D_MODEL // LANES
    acc = x_ref[...]
    for br, o_ref in enumerate((oa_ref, ob_ref, oc_ref)):
        o = jnp.concatenate([o_ref[hh] for hh in range(nb)], axis=1)
        bo = jnp.dot(o, w_ref[br], preferred_element_type=F32)
        gl = jnp.concatenate([g_ref[br * ngb + cbk] for cbk in range(ngb)], axis=1).astype(F32)
        acc = acc + _sigmoid(gl) * bo
    ms = jnp.mean(acc * acc, axis=-1, keepdims=True)
    hn = acc * lax.rsqrt(ms + EPS) * nw_ref[...]
    if final:
        out_refs[0][...] = hn.astype(out_refs[0].dtype)
    else:
        out_refs[0][...] = acc
        out_refs[1][...] = hn.astype(out_refs[1].dtype)


def _merge(oa, ob, oc, gates, x, w_out, nw, layer, final, tm=256):
    nb = WIDTH // LANES
    ngb = GATE_COLS // LANES
    obs = pl.BlockSpec((nb, tm, LANES), lambda i: (0, i, 0))
    row = pl.BlockSpec((tm, D_MODEL), lambda i: (i, 0))
    if final:
        out_specs = row
        out_shape = jax.ShapeDtypeStruct((TOKENS, D_MODEL), F32)
    else:
        out_specs = [row, row]
        out_shape = [jax.ShapeDtypeStruct((TOKENS, D_MODEL), F32),
                     jax.ShapeDtypeStruct((TOKENS, D_MODEL), BF16)]
    return pl.pallas_call(
        functools.partial(_merge_kernel, final),
        grid=(TOKENS // tm,),
        in_specs=[obs, obs, obs,
                  pl.BlockSpec((ngb, tm, LANES), lambda i: (0, i, 0)),
                  row,
                  pl.BlockSpec((None, N_BRANCH, WIDTH, D_MODEL), lambda i: (layer, 0, 0, 0)),
                  pl.BlockSpec((1, D_MODEL), lambda i: (0, 0))],
        out_specs=out_specs,
        out_shape=out_shape,
        compiler_params=_cparams(1),
        name="merge",
    )(oa, ob, oc, gates, x, w_out, nw)


def _blocks(p, rows):
    d, r, n = p.shape
    return p.reshape(d, r, n // LANES, LANES).transpose(0, 2, 1, 3)


def kernel(x, norm_w, w_in, hgrn_lb_logits, hgrn_norm_w, rglru_conv_w, rglru_conv_b, rglru_wa, rglru_ba,
           rglru_wx, rglru_bx, rglru_lambda, ssd_conv_w, ssd_conv_b, ssd_dt_bias, ssd_a_log, ssd_d,
           ssd_norm_w, w_out, final_norm_w):
    w_main = w_in[:, :, :MAIN_COLS].astype(BF16)
    w_gate = w_in[:, :, GATE_COL0:].astype(BF16)
    w_dt = w_in[:, :, DT_COL0:GATE_COL0].reshape(DEPTH, D_MODEL, C_GROUPS, C_GROUP_HEADS)
    w_dt = jnp.pad(w_dt, ((0, 0), (0, 0), (0, 0), (0, LANES - C_GROUP_HEADS)))
    w_dt = w_dt.reshape(DEPTH, D_MODEL, C_GROUPS * LANES).astype(BF16)
    w_o = w_out.reshape(DEPTH, N_BRANCH, WIDTH, D_MODEL).astype(BF16)
    lbl = hgrn_lb_logits.reshape(DEPTH, A_HEADS, LANES).transpose(1, 0, 2)
    a_nw = _blocks(hgrn_norm_w[:, None, :], 1)
    b_cw = _blocks(rglru_conv_w, B_CONV)
    b_cb = _blocks(rglru_conv_b[:, None, :], 1)
    b_wa = rglru_wa.astype(BF16)
    b_wx = rglru_wx.astype(BF16)
    b_ba = rglru_ba[:, :, None, :]
    b_bx = rglru_bx[:, :, None, :]
    b_lam = _blocks(rglru_lambda[:, None, :], 1)
    c_cw = _blocks(ssd_conv_w, C_CONV)
    c_cb = _blocks(ssd_conv_b[:, None, :], 1)
    pad_heads = lambda p: jnp.pad(p.reshape(DEPTH, C_GROUPS, 1, C_GROUP_HEADS),
                                  ((0, 0), (0, 0), (0, 0), (0, LANES - C_GROUP_HEADS)))
    c_dtb = pad_heads(ssd_dt_bias)
    c_alog = pad_heads(ssd_a_log)
    c_dsk = _blocks(jnp.repeat(ssd_d, C_HDIM, axis=1)[:, None, :], 1)
    c_nw = _blocks(ssd_norm_w[:, None, :], 1)

    xf = x.reshape(TOKENS, D_MODEL)
    h = _rmsnorm(xf, norm_w[0:1], BF16)
    for l in range(DEPTH):
        proj = _project(h, w_main, l, BF16, tm=2048, tn=512, name="proj_main")
        gates = _project(h, w_gate, l, BF16, tm=2048, tn=512, name="proj_gate")
        dtp = _project(h, w_dt, l, F32, tm=2048, tn=256, name="proj_dt")
        oa = _hgrn2(proj, lbl, a_nw, l)
        ob = _rglru(proj, b_cw, b_cb, b_wa, b_ba, b_wx, b_bx, b_lam, l)
        oc = _ssd(proj, dtp, c_cw, c_cb, c_dtb, c_alog, c_dsk, c_nw, l)
        if l + 1 < DEPTH:
            xf, h = _merge(oa, ob, oc, gates, xf, w_o, norm_w[l + 1:l + 2], l, final=False)
        else:
            out = _merge(oa, ob, oc, gates, xf, w_o, final_norm_w[None, :], l, final=True)
    return out.reshape(BATCH, SEQ, D_MODEL)
```

```python
import functools

import jax
import jax.numpy as jnp
from jax import lax
from jax.experimental import pallas as pl
from jax.experimental.pallas import tpu as pltpu

F32 = jnp.float32
BF16 = jnp.bfloat16

D_MODEL = 2048
BATCH = 2
SEQ = 4096
DEPTH = 4
TOKENS = BATCH * SEQ
EPS = 1e-6
WIDTH = 1024
N_BRANCH = 3
LANES = 128
SUBLANES = 8
A_HEADS = 8
A_CHUNK = 64
A_MAX_DECAY = 80.0
B_HEADS = 8
B_CONV = 4
B_C = 8.0
C_HDIM = 64
C_HEADS = 16
C_GROUPS = 2
C_STATE = 128
C_CONV = 4
C_CHUNK = 128
C_GROUP_W = WIDTH // C_GROUPS
C_GROUP_HEADS = C_HEADS // C_GROUPS
C_XBLOCKS = C_GROUP_W // LANES
IN_COLS = 14864
MAIN_COLS = 8 * WIDTH + 2 * C_GROUPS * C_STATE
GATE_COL0 = MAIN_COLS + C_HEADS
GATE_SHIFT = GATE_COL0 - MAIN_COLS
PROJ_TN = 512
PROJ_FULL_TILES = IN_COLS // PROJ_TN
PROJ_MAIN_TILES = MAIN_COLS // PROJ_TN
TAIL_COL0 = PROJ_FULL_TILES * PROJ_TN
TILE_BLOCKS = PROJ_TN // LANES
GATE_BLOCKS = (PROJ_FULL_TILES - PROJ_MAIN_TILES) * TILE_BLOCKS
MAIN_BLK0 = GATE_BLOCKS
BLK_AQ, BLK_AF, BLK_AI, BLK_AZ = MAIN_BLK0, MAIN_BLK0 + 8, MAIN_BLK0 + 16, MAIN_BLK0 + 24
BLK_BX, BLK_BZ = MAIN_BLK0 + 32, MAIN_BLK0 + 40
BLK_CZ, BLK_CX, BLK_CB, BLK_CC = MAIN_BLK0 + 48, MAIN_BLK0 + 56, MAIN_BLK0 + 64, MAIN_BLK0 + 66
PROJ_BLOCKS = MAIN_BLK0 + MAIN_COLS // LANES
OUT_PAD = D_MODEL + LANES
CONV_HALO = SUBLANES
VMEM_LIMIT = 56 * 1024 * 1024

_NT = (((1,), (1,)), ((), ()))
_TN = (((0,), (0,)), ((), ()))


def _cparams(n_axes):
    return pltpu.CompilerParams(dimension_semantics=("arbitrary",) * n_axes,
                                vmem_limit_bytes=VMEM_LIMIT)


def _sigmoid(x):
    return 1.0 / (1.0 + jnp.exp(-x))


def _silu(x):
    return x * _sigmoid(x)


def _softplus(x):
    return jnp.maximum(x, 0.0) + jnp.log1p(jnp.exp(-jnp.abs(x)))


def _dot_exact_lhs(m01, x):
    x1 = x.astype(BF16)
    r1 = x - x1.astype(F32)
    x2 = r1.astype(BF16)
    x3 = (r1 - x2.astype(F32)).astype(BF16)
    d = lambda a: jnp.dot(m01, a, preferred_element_type=F32)
    return d(x1) + d(x2) + d(x3)


def _dot_hilo_rhs(x, m01_twice):
    x1 = x.astype(BF16)
    x2 = (x - x1.astype(F32)).astype(BF16)
    return jnp.dot(jnp.concatenate([x1, x2], axis=1), m01_twice, preferred_element_type=F32)


def _tril(n):
    r = lax.broadcasted_iota(jnp.int32, (n, n), 0)
    c = lax.broadcasted_iota(jnp.int32, (n, n), 1)
    return r >= c


def _rms_kernel(x_ref, w_ref, o_ref):
    x = x_ref[...]
    ms = jnp.mean(x * x, axis=-1, keepdims=True)
    o_ref[...] = (x * lax.rsqrt(ms + EPS) * w_ref[...]).astype(o_ref.dtype)


def _rmsnorm(x, w, out_dtype, tm=512):
    t, d = x.shape
    return pl.pallas_call(
        _rms_kernel,
        grid=(t // tm,),
        in_specs=[pl.BlockSpec((tm, d), lambda i: (i, 0)),
                  pl.BlockSpec((1, d), lambda i: (0, 0))],
        out_specs=pl.BlockSpec((tm, d), lambda i: (i, 0)),
        out_shape=jax.ShapeDtypeStruct((t, d), out_dtype),
        compiler_params=_cparams(1),
        name="rmsnorm",
    )(x, w)


def _proj_kernel(h_ref, w_ref, wt_ref, o_ref, dt_ref, gt_ref):
    j = pl.program_id(1)
    h = h_ref[...]
    acc = jnp.dot(h, w_ref[...].astype(BF16), preferred_element_type=F32)
    for c in range(TILE_BLOCKS):
        o_ref[c] = acc[:, c * LANES:(c + 1) * LANES].astype(o_ref.dtype)

    @pl.when(j == PROJ_MAIN_TILES)
    def _():
        dt_ref[0] = acc[:, :LANES]

    @pl.when(j == PROJ_FULL_TILES - 1)
    def _():
        gt_ref[0] = jnp.dot(h, wt_ref[...], preferred_element_type=F32).astype(gt_ref.dtype)


def _project(h, w_in, w_tail, layer, tm=2048):
    t, d = h.shape
    row = lambda i, j: (0, i, 0)
    return pl.pallas_call(
        _proj_kernel,
        grid=(t // tm, PROJ_FULL_TILES),
        in_specs=[pl.BlockSpec((tm, d), lambda i, j: (i, 0)),
                  pl.BlockSpec((None, d, PROJ_TN), lambda i, j: (layer, 0, j)),
                  pl.BlockSpec((None, d, LANES), lambda i, j: (layer, 0, 0))],
        out_specs=[pl.BlockSpec((TILE_BLOCKS, tm, LANES),
                                lambda i, j: ((j + PROJ_FULL_TILES - PROJ_MAIN_TILES) % PROJ_FULL_TILES, i, 0)),
                   pl.BlockSpec((1, tm, LANES), row),
                   pl.BlockSpec((1, tm, LANES), row)],
        out_shape=[jax.ShapeDtypeStruct((PROJ_BLOCKS, t, LANES), BF16),
                   jax.ShapeDtypeStruct((1, t, LANES), F32),
                   jax.ShapeDtypeStruct((1, t, LANES), BF16)],
        compiler_params=_cparams(2),
        name="proj_in",
    )(h, w_in, w_tail)


def _hgrn_kernel(layer, q_ref, f_ref, i_ref, z_ref, lbl_ref, nw_ref, o_ref, st_ref, b_ref, k_ref, a_ref):
    tt = q_ref.shape[1]

    @pl.when(pl.program_id(2) == 0)
    def _():
        st_ref[...] = jnp.zeros_like(st_ref)

    lbl = lbl_ref[0]
    e = jnp.exp(lbl - jnp.max(lbl, axis=0, keepdims=True))
    lb = jnp.zeros((1, LANES), F32)
    for j in range(1, layer + 1):
        lb = lb + e[j:j + 1, :]
    lb = lb / jnp.sum(e, axis=0, keepdims=True)
    nw = nw_ref[0]

    nc = tt // A_CHUNK
    chunks = [slice(c * A_CHUNK, (c + 1) * A_CHUNK) for c in range(nc)]
    fx = f_ref[0].astype(F32)
    f = lb + (1.0 - lb) * _sigmoid(fx)
    g = jnp.maximum(jnp.log(f), -128.0)
    kk = 1.0 - f
    rc = lax.broadcasted_iota(jnp.int32, (tt, LANES), 0) & (A_CHUNK - 1)
    b = g
    s = 1
    while s < A_CHUNK:
        b = b + jnp.where(rc >= s, pltpu.roll(b, s, 0), 0.0)
        s *= 2
    q = q_ref[0].astype(F32)
    qs = (q * jnp.exp(b)).astype(BF16)
    b_ref[...] = b
    k_ref[...] = kk
    causal = _tril(A_CHUNK)
    small_decay = jnp.min(b) >= -A_MAX_DECAY

    @pl.when(small_decay)
    def _():
        kt = (kk * jnp.exp(-b)).astype(BF16)
        for rows in chunks:
            a = lax.dot_general(qs[rows], kt[rows], _NT, preferred_element_type=F32)
            a_ref[rows, :] = jnp.where(causal, a, 0.0)

    @pl.when(jnp.logical_not(small_decay))
    def _():
        coli = lax.broadcasted_iota(jnp.int32, (A_CHUNK, A_CHUNK), 1)

        def chunk_body(c, carry):
            r0 = pl.multiple_of(c * A_CHUNK, A_CHUNK)
            rows = pl.ds(r0, A_CHUNK)
            bc = b_ref[rows, :]
            qc = q_ref[0, rows, :].astype(F32)

            def col_body(j, acc):
                bj = b_ref[pl.ds(r0 + j, 1), :]
                kj = k_ref[pl.ds(r0 + j, 1), :]
                p = qc * jnp.exp(jnp.minimum(bc - bj, 0.0)) * kj
                return jnp.where(coli == j, jnp.sum(p, axis=-1, keepdims=True), acc)

            acc = lax.fori_loop(0, A_CHUNK, col_body, jnp.zeros((A_CHUNK, A_CHUNK), F32))
            a_ref[rows, :] = jnp.where(causal, acc, 0.0)
            return carry

        lax.fori_loop(0, nc, chunk_body, 0)

    v = i_ref[0]
    b_last = [b[r.stop - 1:r.stop, :] for r in chunks]
    bl_full = jnp.concatenate([jnp.broadcast_to(bl, (A_CHUNK, LANES)) for bl in b_last], axis=0)
    kd = (kk * jnp.exp(bl_full - b)).astype(BF16)
    o_intra = [jnp.dot(a_ref[r, :].astype(BF16), v[r], preferred_element_type=F32) for r in chunks]
    kv = [lax.dot_general(v[r], kd[r], _TN, preferred_element_type=F32) for r in chunks]
    st = st_ref[...]
    outs = []
    for c, r in enumerate(chunks):
        outs.append(o_intra[c] + lax.dot_general(qs[r], st.astype(BF16), _NT, preferred_element_type=F32))
        st = st * jnp.exp(b_last[c]) + kv[c]
    st_ref[...] = st
    o = jnp.concatenate(outs, axis=0)
    ms = jnp.mean(o * o, axis=-1, keepdims=True)
    y = o * lax.rsqrt(ms + EPS) * nw
    o_ref[0] = (y * _silu(z_ref[0].astype(F32))).astype(o_ref.dtype)


def _hgrn2(proj, lbl, nw, layer, tt=512):
    nt = SEQ // tt
    blk = lambda off: pl.BlockSpec((1, tt, LANES), lambda b, h, t: (off + h, b * nt + t, 0))
    return pl.pallas_call(
        functools.partial(_hgrn_kernel, layer),
        grid=(BATCH, A_HEADS, nt),
        in_specs=[blk(BLK_AQ), blk(BLK_AF), blk(BLK_AI), blk(BLK_AZ),
                  pl.BlockSpec((1, DEPTH, LANES), lambda b, h, t: (h, 0, 0)),
                  pl.BlockSpec((None, 1, 1, LANES), lambda b, h, t: (layer, h, 0, 0))],
        out_specs=pl.BlockSpec((1, tt, LANES), lambda b, h, t: (h, b * nt + t, 0)),
        out_shape=jax.ShapeDtypeStruct((A_HEADS, TOKENS, LANES), BF16),
        scratch_shapes=[pltpu.VMEM((LANES, LANES), F32),
                        pltpu.VMEM((tt, LANES), F32),
                        pltpu.VMEM((tt, LANES), F32),
                        pltpu.VMEM((tt, A_CHUNK), F32)],
        compiler_params=_cparams(3),
        name="hgrn2",
    )(proj, proj, proj, proj, lbl, nw)


def _causal_conv(xbuf, x, cw, cb, first):
    tt = x.shape[0]

    @pl.when(first)
    def _():
        xbuf[0:CONV_HALO, :] = jnp.zeros((CONV_HALO, LANES), F32)

    xbuf[CONV_HALO:CONV_HALO + tt, :] = x
    y = cb
    for k in range(B_CONV):
        y = y + cw[k:k + 1, :] * xbuf[pl.ds(CONV_HALO - (B_CONV - 1) + k, tt), :]
    xbuf[0:CONV_HALO, :] = x[tt - CONV_HALO:tt, :]
    return y


def _rglru_kernel(x_ref, z_ref, cw_ref, cb_ref, wa_ref, ba_ref, wx_ref, bx_ref, lam_ref, o_ref,
                  xbuf, a_s, u_s, hcar):
    tt = x_ref.shape[1]
    first = pl.program_id(1) == 0

    @pl.when(first)
    def _():
        hcar[...] = jnp.zeros_like(hcar)

    rowm = lax.broadcasted_iota(jnp.int32, (tt, LANES), 0) & (SUBLANES - 1)
    for h in range(B_HEADS):
        xc = _causal_conv(xbuf.at[h], x_ref[h].astype(F32), cw_ref[h], cb_ref[h], first)
        xcb = xc.astype(BF16)
        r = _sigmoid(jnp.dot(xcb, wa_ref[h], preferred_element_type=F32) + ba_ref[h])
        ig = _sigmoid(jnp.dot(xcb, wx_ref[h], preferred_element_type=F32) + bx_ref[h])
        log_a = (-B_C) * r * _softplus(-lam_ref[h])
        a = jnp.exp(log_a)
        u = jnp.sqrt(1.0 - jnp.exp(2.0 * log_a)) * (ig * xc)
        for s in (1, 2, 4):
            a_sh = pltpu.roll(a, s, 0)
            u_sh = pltpu.roll(u, s, 0)
            m = rowm >= s
            u = jnp.where(m, a * u_sh + u, u)
            a = jnp.where(m, a * a_sh, a)
        a_s[h] = a
        u_s[h] = u

    def grp(g, hs):
        rows = pl.ds(pl.multiple_of(g * SUBLANES, SUBLANES), SUBLANES)
        new = []
        for h in range(B_HEADS):
            hg = a_s[h, rows, :] * hs[h] + u_s[h, rows, :]
            u_s[h, rows, :] = hg
            new.append(jnp.broadcast_to(hg[SUBLANES - 1:SUBLANES, :], (SUBLANES, LANES)))
        return tuple(new)

    hs = lax.fori_loop(0, tt // SUBLANES, grp, tuple(hcar[h] for h in range(B_HEADS)), unroll=4)
    for h in range(B_HEADS):
        hcar[h] = hs[h]
        o_ref[h] = (u_s[h] * _silu(z_ref[h].astype(F32))).astype(o_ref.dtype)


def _rglru(proj, cw, cb, wa, ba, wx, bx, lam, layer, tt=512):
    nt = SEQ // tt
    blk = lambda off: pl.BlockSpec((B_HEADS, tt, LANES), lambda b, t: (off // B_HEADS, b * nt + t, 0))
    par = lambda r, c: pl.BlockSpec((None, B_HEADS, r, c), lambda b, t: (layer, 0, 0, 0))
    return pl.pallas_call(
        _rglru_kernel,
        grid=(BATCH, nt),
        in_specs=[blk(BLK_BX), blk(BLK_BZ), par(B_CONV, LANES), par(1, LANES),
                  par(LANES, LANES), par(1, LANES), par(LANES, LANES), par(1, LANES), par(1, LANES)],
        out_specs=pl.BlockSpec((B_HEADS, tt, LANES), lambda b, t: (0, b * nt + t, 0)),
        out_shape=jax.ShapeDtypeStruct((B_HEADS, TOKENS, LANES), BF16),
        scratch_shapes=[pltpu.VMEM((B_HEADS, tt + CONV_HALO, LANES), F32),
                        pltpu.VMEM((B_HEADS, tt, LANES), F32),
                        pltpu.VMEM((B_HEADS, tt, LANES), F32),
                        pltpu.VMEM((B_HEADS, SUBLANES, LANES), F32)],
        compiler_params=_cparams(2),
        name="rglru",
    )(proj, proj, cw, cb, wa, ba, wx, bx, lam)


def _ssd_kernel(z_ref, x_ref, bm_ref, cm_ref, dt_ref, cwx_ref, cwb_ref, cwc_ref, cbx_ref, cbb_ref, cbc_ref,
                dtb_ref, alog_ref, dsk_ref, nw_ref, o_ref, xbuf, st_ref):
    tt = x_ref.shape[1]
    nxb = C_XBLOCKS
    first = pl.program_id(1) == 0

    @pl.when(first)
    def _():
        st_ref[...] = jnp.zeros_like(st_ref)

    dt = _softplus(dt_ref[0] + dtb_ref[...])
    da = dt * (-jnp.exp(alog_ref[...]))
    causal = _tril(tt)
    cum = _dot_exact_lhs(causal.astype(BF16), da)
    cum_t = cum.T
    dt_t = dt.T
    ecum = jnp.exp(cum)
    cl = cum[tt - 1:tt, :]
    w = dt * jnp.exp(cl - cum)
    ecl = jnp.broadcast_to(jnp.exp(cl), (SUBLANES, LANES))
    lane = lax.broadcasted_iota(jnp.int32, (tt, LANES), 1)
    er = lax.broadcasted_iota(jnp.int32, (2 * LANES, C_GROUP_W), 0) & (LANES - 1)
    ec = lax.broadcasted_iota(jnp.int32, (2 * LANES, C_GROUP_W), 1)

    for g in range(C_GROUPS):
        xs_blocks = [_silu(_causal_conv(xbuf.at[g * nxb + p], x_ref[g * nxb + p].astype(F32),
                                        cwx_ref[g * nxb + p], cbx_ref[g * nxb + p], first)) for p in range(nxb)]
        bm = _silu(_causal_conv(xbuf.at[2 * nxb + g], bm_ref[g].astype(F32), cwb_ref[g], cbb_ref[g], first))
        cm = _silu(_causal_conv(xbuf.at[2 * nxb + C_GROUPS + g], cm_ref[g].astype(F32), cwc_ref[g], cbc_ref[g], first))
        cmb = cm.astype(BF16)
        bmb = bm.astype(BF16)
        cb = lax.dot_general(cmb, bmb, _NT, preferred_element_type=F32)
        y_blocks = []
        for p in range(nxb):
            xb = xs_blocks[p]
            acc = None
            for half in range(2):
                hl = g * C_GROUP_HEADS + 2 * p + half
                seg = cum[:, hl:hl + 1] - cum_t[hl:hl + 1, :]
                lmat = jnp.exp(jnp.where(causal, seg, -1e30))
                m = (cb * lmat * dt_t[hl:hl + 1, :]).astype(BF16)
                in_head = (lane >= C_HDIM) if half else (lane < C_HDIM)
                rhs = jnp.where(in_head, xb, 0.0).astype(BF16)
                part = jnp.dot(m, rhs, preferred_element_type=F32)
                acc = part if acc is None else acc + part
            y_blocks.append(acc)
        xs = jnp.concatenate(xs_blocks, axis=1)
        y = jnp.concatenate(y_blocks, axis=1)
        ej = er - g * C_GROUP_HEADS
        spread = ((ec >= ej * C_HDIM) & (ec < (ej + 1) * C_HDIM)).astype(BF16)
        st = st_ref[g]
        y = y + jnp.dot(cmb, st.astype(BF16), preferred_element_type=F32) * _dot_hilo_rhs(ecum, spread)
        y = y + jnp.concatenate([dsk_ref[g * nxb + p] for p in range(nxb)], axis=1) * xs
        xw = (xs * _dot_hilo_rhs(w, spread)).astype(BF16)
        dec = _dot_hilo_rhs(ecl, spread)[0:1, :]
        st_ref[g] = st * dec + lax.dot_general(bmb, xw, _TN, preferred_element_type=F32)
        zf = jnp.concatenate([z_ref[g * nxb + p].astype(F32) for p in range(nxb)], axis=1)
        yg = y * _silu(zf)
        ms = jnp.mean(yg * yg, axis=-1, keepdims=True)
        out = yg * lax.rsqrt(ms + EPS) * jnp.concatenate([nw_ref[g * nxb + p] for p in range(nxb)], axis=1)
        for p in range(nxb):
            o_ref[g * nxb + p] = out[:, p * LANES:(p + 1) * LANES].astype(o_ref.dtype)


def _ssd(proj, dtp, cw, cb, dtb, alog, dsk, nw, layer):
    tt = C_CHUNK
    nt = SEQ // tt
    nx = C_GROUPS * C_XBLOCKS
    rows = lambda b, t: b * nt + t
    big = lambda off: pl.BlockSpec((nx, tt, LANES), lambda b, t: (off // nx, rows(b, t), 0))
    two = lambda off: pl.BlockSpec((C_GROUPS, tt, LANES), lambda b, t: (off // C_GROUPS, rows(b, t), 0))
    par = lambda n, r, blk: pl.BlockSpec((None, n, r, LANES), lambda b, t: (layer, blk, 0, 0))
    vec = pl.BlockSpec((None, 1, LANES), lambda b, t: (layer, 0, 0))
    return pl.pallas_call(
        _ssd_kernel,
        grid=(BATCH, nt),
        in_specs=[big(BLK_CZ), big(BLK_CX), two(BLK_CB), two(BLK_CC),
                  pl.BlockSpec((1, tt, LANES), lambda b, t: (0, rows(b, t), 0)),
                  par(nx, C_CONV, 0), par(C_GROUPS, C_CONV, nx // C_GROUPS), par(C_GROUPS, C_CONV, nx // C_GROUPS + 1),
                  par(nx, 1, 0), par(C_GROUPS, 1, nx // C_GROUPS), par(C_GROUPS, 1, nx // C_GROUPS + 1),
                  vec, vec, par(nx, 1, 0), par(nx, 1, 0)],
        out_specs=pl.BlockSpec((nx, tt, LANES), lambda b, t: (0, rows(b, t), 0)),
        out_shape=jax.ShapeDtypeStruct((nx, TOKENS, LANES), BF16),
        scratch_shapes=[pltpu.VMEM((nx + 2 * C_GROUPS, tt + CONV_HALO, LANES), F32),
                        pltpu.VMEM((C_GROUPS, C_STATE, C_GROUP_W), F32)],
        compiler_params=_cparams(2),
        name="ssd",
    )(proj, proj, proj, proj, dtp, cw, cw, cw, cb, cb, cb, dtb, alog, dsk, nw)


def _merge_kernel(final, oa_ref, ob_ref, oc_ref, g_ref, gt_ref, x_ref, w_ref, nw_ref, *out_refs):
    nb = WIDTH // LANES
    ngb = D_MODEL // LANES
    acc = None
    for br, o_ref in enumerate((oa_ref, ob_ref, oc_ref)):
        o = jnp.concatenate([o_ref[hh] for hh in range(nb)], axis=1)
        bo = jnp.dot(o, w_ref[br], preferred_element_type=F32)
        last = g_ref[(br + 1) * ngb] if br + 1 < N_BRANCH else gt_ref[0]
        gl = jnp.concatenate([g_ref[br * ngb + c] for c in range(ngb)] + [last], axis=1).astype(F32)
        part = _sigmoid(gl) * bo
        acc = part if acc is None else acc + part
    xn = x_ref[...] + pltpu.roll(acc, OUT_PAD - GATE_SHIFT, 1)[:, :D_MODEL]
    ms = jnp.mean(xn * xn, axis=-1, keepdims=True)
    hn = xn * lax.rsqrt(ms + EPS) * nw_ref[...]
    if final:
        out_refs[0][...] = hn.astype(out_refs[0].dtype)
    else:
        out_refs[0][...] = xn
        out_refs[1][...] = hn.astype(out_refs[1].dtype)


def _merge(oa, ob, oc, proj, gtail, x, w_out, nw, layer, final, tm=256):
    nb = WIDTH // LANES
    obs = pl.BlockSpec((nb, tm, LANES), lambda i: (0, i, 0))
    row = pl.BlockSpec((tm, D_MODEL), lambda i: (i, 0))
    if final:
        out_specs = row
        out_shape = jax.ShapeDtypeStruct((TOKENS, D_MODEL), F32)
    else:
        out_specs = [row, row]
        out_shape = [jax.ShapeDtypeStruct((TOKENS, D_MODEL), F32),
                     jax.ShapeDtypeStruct((TOKENS, D_MODEL), BF16)]
    return pl.pallas_call(
        functools.partial(_merge_kernel, final),
        grid=(TOKENS // tm,),
        in_specs=[obs, obs, obs,
                  pl.BlockSpec((GATE_BLOCKS, tm, LANES), lambda i: (0, i, 0)),
                  pl.BlockSpec((1, tm, LANES), lambda i: (0, i, 0)),
                  row,
                  pl.BlockSpec((None, N_BRANCH, WIDTH, OUT_PAD), lambda i: (layer, 0, 0, 0)),
                  pl.BlockSpec((1, D_MODEL), lambda i: (0, 0))],
        out_specs=out_specs,
        out_shape=out_shape,
        compiler_params=_cparams(1),
        name="merge",
    )(oa, ob, oc, proj, gtail, x, w_out, nw)


def _blocks(p, rows):
    d, r, n = p.shape
    return p.reshape(d, r, n // LANES, LANES).transpose(0, 2, 1, 3)


def _pad_lanes(p):
    return jnp.pad(p, ((0, 0),) * (p.ndim - 1) + ((0, LANES - p.shape[-1]),))


def kernel(x, norm_w, w_in, hgrn_lb_logits, hgrn_norm_w, rglru_conv_w, rglru_conv_b, rglru_wa, rglru_ba,
           rglru_wx, rglru_bx, rglru_lambda, ssd_conv_w, ssd_conv_b, ssd_dt_bias, ssd_a_log, ssd_d,
           ssd_norm_w, w_out, final_norm_w):
    w_tail = _pad_lanes(w_in[:, :, TAIL_COL0:]).astype(BF16)
    w_o = jnp.pad(w_out.reshape(DEPTH, N_BRANCH, WIDTH, D_MODEL),
                  ((0, 0), (0, 0), (0, 0), (GATE_SHIFT, OUT_PAD - D_MODEL - GATE_SHIFT))).astype(BF16)
    lbl = hgrn_lb_logits.reshape(DEPTH, A_HEADS, LANES).transpose(1, 0, 2)
    a_nw = _blocks(hgrn_norm_w[:, None, :], 1)
    b_cw = _blocks(rglru_conv_w, B_CONV)
    b_cb = _blocks(rglru_conv_b[:, None, :], 1)
    b_wa = rglru_wa.astype(BF16)
    b_wx = rglru_wx.astype(BF16)
    b_ba = rglru_ba[:, :, None, :]
    b_bx = rglru_bx[:, :, None, :]
    b_lam = _blocks(rglru_lambda[:, None, :], 1)
    c_cw = _blocks(ssd_conv_w, C_CONV)
    c_cb = _blocks(ssd_conv_b[:, None, :], 1)
    c_dtb = _pad_lanes(ssd_dt_bias)[:, None, :]
    c_alog = _pad_lanes(ssd_a_log)[:, None, :]
    c_dsk = _blocks(jnp.repeat(ssd_d, C_HDIM, axis=1)[:, None, :], 1)
    c_nw = _blocks(ssd_norm_w[:, None, :], 1)

    xf = x.reshape(TOKENS, D_MODEL)
    h = _rmsnorm(xf, norm_w[0:1], BF16)
    for l in range(DEPTH):
        proj, dtp, gtail = _project(h, w_in, w_tail, l)
        oa = _hgrn2(proj, lbl, a_nw, l)
        ob = _rglru(proj, b_cw, b_cb, b_wa, b_ba, b_wx, b_bx, b_lam, l)
        oc = _ssd(proj, dtp, c_cw, c_cb, c_dtb, c_alog, c_dsk, c_nw, l)
        if l + 1 < DEPTH:
            xf, h = _merge(oa, ob, oc, proj, gtail, xf, w_o, norm_w[l + 1:l + 2], l, final=False)
        else:
            out = _merge(oa, ob, oc, proj, gtail, xf, w_o, final_norm_w[None, :], l, final=True)
    return out.reshape(BATCH, SEQ, D_MODEL)
```

```python
import functools

import jax
import jax.numpy as jnp
from jax import lax
from jax.experimental import pallas as pl
from jax.experimental.pallas import tpu as pltpu

F32 = jnp.float32
BF16 = jnp.bfloat16

D_MODEL = 2048
BATCH = 2
SEQ = 4096
DEPTH = 4
TOKENS = BATCH * SEQ
EPS = 1e-6
WIDTH = 1024
N_BRANCH = 3
LANES = 128
SUBLANES = 8
A_HEADS = 8
A_CHUNK = 64
A_MAX_DECAY = 80.0
B_HEADS = 8
B_CONV = 4
B_C = 8.0
C_HDIM = 64
C_HEADS = 16
C_GROUPS = 2
C_STATE = 128
C_CONV = 4
C_CHUNK = 128
C_GROUP_W = WIDTH // C_GROUPS
C_GROUP_HEADS = C_HEADS // C_GROUPS
C_XBLOCKS = C_GROUP_W // LANES
MAIN_COLS = 8 * WIDTH + 2 * C_GROUPS * C_STATE
GATE_COL0 = MAIN_COLS + C_HEADS
PROJ_TN = 512
PROJ_TILES = MAIN_COLS // PROJ_TN
TILE_BLOCKS = PROJ_TN // LANES
BLK_AQ, BLK_AF, BLK_AI, BLK_AZ = 0, 8, 16, 24
BLK_BX, BLK_BZ = 32, 40
BLK_CZ, BLK_CX, BLK_CB, BLK_CC = 48, 56, 64, 66
MAIN_BLOCKS = MAIN_COLS // LANES
GATE_BLOCKS = D_MODEL // LANES
CONV_HALO = SUBLANES
VMEM_LIMIT = 56 * 1024 * 1024

_NT = (((1,), (1,)), ((), ()))
_TN = (((0,), (0,)), ((), ()))


def _cparams(n_axes):
    return pltpu.CompilerParams(dimension_semantics=("arbitrary",) * n_axes,
                                vmem_limit_bytes=VMEM_LIMIT)


def _sigmoid(x):
    return 1.0 / (1.0 + jnp.exp(-x))


def _silu(x):
    return x * _sigmoid(x)


def _softplus(x):
    return jnp.maximum(x, 0.0) + jnp.log1p(jnp.exp(-jnp.abs(x)))


def _dot_exact_lhs(m01, x):
    x1 = x.astype(BF16)
    r1 = x - x1.astype(F32)
    x2 = r1.astype(BF16)
    x3 = (r1 - x2.astype(F32)).astype(BF16)
    d = lambda a: jnp.dot(m01, a, preferred_element_type=F32)
    return d(x1) + d(x2) + d(x3)


def _dot_hilo_rhs(x, m01_twice):
    x1 = x.astype(BF16)
    x2 = (x - x1.astype(F32)).astype(BF16)
    return jnp.dot(jnp.concatenate([x1, x2], axis=1), m01_twice, preferred_element_type=F32)


def _tril(n):
    r = lax.broadcasted_iota(jnp.int32, (n, n), 0)
    c = lax.broadcasted_iota(jnp.int32, (n, n), 1)
    return r >= c


def _store_gate_logits(g_ref, h, wg):
    gl = lax.dot_general(h, wg, _NT, preferred_element_type=F32)
    for c in range(g_ref.shape[0]):
        g_ref[c] = gl[:, c * LANES:(c + 1) * LANES].astype(g_ref.dtype)


def _rms_kernel(x_ref, w_ref, o_ref):
    x = x_ref[...]
    ms = jnp.mean(x * x, axis=-1, keepdims=True)
    o_ref[...] = (x * lax.rsqrt(ms + EPS) * w_ref[...]).astype(o_ref.dtype)


def _rmsnorm(x, w, out_dtype, tm=512):
    t, d = x.shape
    return pl.pallas_call(
        _rms_kernel,
        grid=(t // tm,),
        in_specs=[pl.BlockSpec((tm, d), lambda i: (i, 0)),
                  pl.BlockSpec((1, d), lambda i: (0, 0))],
        out_specs=pl.BlockSpec((tm, d), lambda i: (i, 0)),
        out_shape=jax.ShapeDtypeStruct((t, d), out_dtype),
        compiler_params=_cparams(1),
        name="rmsnorm",
    )(x, w)


def _proj_kernel(h_ref, w_ref, wdt_ref, o_ref, dt_ref):
    h = h_ref[...]
    acc = lax.dot_general(h, w_ref[...].astype(BF16), _NT, preferred_element_type=F32)
    for c in range(TILE_BLOCKS):
        o_ref[c] = acc[:, c * LANES:(c + 1) * LANES].astype(o_ref.dtype)

    @pl.when(pl.program_id(1) == PROJ_TILES - 1)
    def _():
        dt_ref[0] = lax.dot_general(h, wdt_ref[...].astype(BF16), _NT, preferred_element_type=F32)


def _project(h, w_t, layer, tm=2048):
    t, d = h.shape
    return pl.pallas_call(
        _proj_kernel,
        grid=(t // tm, PROJ_TILES),
        in_specs=[pl.BlockSpec((tm, d), lambda i, j: (i, 0)),
                  pl.BlockSpec((None, PROJ_TN, d), lambda i, j: (layer, j, 0)),
                  pl.BlockSpec((None, LANES, d), lambda i, j: (layer, MAIN_COLS // LANES, 0))],
        out_specs=[pl.BlockSpec((TILE_BLOCKS, tm, LANES), lambda i, j: (j, i, 0)),
                   pl.BlockSpec((1, tm, LANES), lambda i, j: (0, i, 0))],
        out_shape=[jax.ShapeDtypeStruct((MAIN_BLOCKS, t, LANES), BF16),
                   jax.ShapeDtypeStruct((1, t, LANES), F32)],
        compiler_params=_cparams(2),
        name="proj_in",
    )(h, w_t, w_t)


def _hgrn_kernel(layer, q_ref, f_ref, i_ref, z_ref, lbl_ref, nw_ref, h_ref, wg_ref, o_ref, g_ref,
                 st_ref, b_ref, k_ref, a_ref):
    tt = q_ref.shape[1]

    @pl.when(pl.program_id(2) == 0)
    def _():
        st_ref[...] = jnp.zeros_like(st_ref)

    _store_gate_logits(g_ref, h_ref[...], wg_ref[...])

    lbl = lbl_ref[0]
    e = jnp.exp(lbl - jnp.max(lbl, axis=0, keepdims=True))
    lb = jnp.zeros((1, LANES), F32)
    for j in range(1, layer + 1):
        lb = lb + e[j:j + 1, :]
    lb = lb / jnp.sum(e, axis=0, keepdims=True)
    nw = nw_ref[0]

    nc = tt // A_CHUNK
    chunks = [slice(c * A_CHUNK, (c + 1) * A_CHUNK) for c in range(nc)]
    fx = f_ref[0].astype(F32)
    f = lb + (1.0 - lb) * _sigmoid(fx)
    g = jnp.maximum(jnp.log(f), -128.0)
    kk = 1.0 - f
    rc = lax.broadcasted_iota(jnp.int32, (tt, LANES), 0) & (A_CHUNK - 1)
    b = g
    s = 1
    while s < A_CHUNK:
        b = b + jnp.where(rc >= s, pltpu.roll(b, s, 0), 0.0)
        s *= 2
    q = q_ref[0].astype(F32)
    qs = (q * jnp.exp(b)).astype(BF16)
    b_ref[...] = b
    k_ref[...] = kk
    causal = _tril(A_CHUNK)
    small_decay = jnp.min(b) >= -A_MAX_DECAY

    @pl.when(small_decay)
    def _():
        kt = (kk * jnp.exp(-b)).astype(BF16)
        for rows in chunks:
            a = lax.dot_general(qs[rows], kt[rows], _NT, preferred_element_type=F32)
            a_ref[rows, :] = jnp.where(causal, a, 0.0)

    @pl.when(jnp.logical_not(small_decay))
    def _():
        coli = lax.broadcasted_iota(jnp.int32, (A_CHUNK, A_CHUNK), 1)

        def chunk_body(c, carry):
            r0 = pl.multiple_of(c * A_CHUNK, A_CHUNK)
            rows = pl.ds(r0, A_CHUNK)
            bc = b_ref[rows, :]
            qc = q_ref[0, rows, :].astype(F32)

            def col_body(j, acc):
                bj = b_ref[pl.ds(r0 + j, 1), :]
                kj = k_ref[pl.ds(r0 + j, 1), :]
                p = qc * jnp.exp(jnp.minimum(bc - bj, 0.0)) * kj
                return jnp.where(coli == j, jnp.sum(p, axis=-1, keepdims=True), acc)

            acc = lax.fori_loop(0, A_CHUNK, col_body, jnp.zeros((A_CHUNK, A_CHUNK), F32))
            a_ref[rows, :] = jnp.where(causal, acc, 0.0)
            return carry

        lax.fori_loop(0, nc, chunk_body, 0)

    v = i_ref[0]
    b_last = [b[r.stop - 1:r.stop, :] for r in chunks]
    bl_full = jnp.concatenate([jnp.broadcast_to(bl, (A_CHUNK, LANES)) for bl in b_last], axis=0)
    kd = (kk * jnp.exp(bl_full - b)).astype(BF16)
    o_intra = [jnp.dot(a_ref[r, :].astype(BF16), v[r], preferred_element_type=F32) for r in chunks]
    kv = [lax.dot_general(v[r], kd[r], _TN, preferred_element_type=F32) for r in chunks]
    st = st_ref[...]
    outs = []
    for c, r in enumerate(chunks):
        outs.append(o_intra[c] + lax.dot_general(qs[r], st.astype(BF16), _NT, preferred_element_type=F32))
        st = st * jnp.exp(b_last[c]) + kv[c]
    st_ref[...] = st
    o = jnp.concatenate(outs, axis=0)
    ms = jnp.mean(o * o, axis=-1, keepdims=True)
    y = o * lax.rsqrt(ms + EPS) * nw
    o_ref[0] = (y * _silu(z_ref[0].astype(F32))).astype(o_ref.dtype)


def _hgrn2(proj, h, wg_t, lbl, nw, layer, tt=512):
    nt = SEQ // tt
    slab = GATE_BLOCKS // A_HEADS
    blk = lambda off: pl.BlockSpec((1, tt, LANES), lambda b, hd, t: (off + hd, b * nt + t, 0))
    return pl.pallas_call(
        functools.partial(_hgrn_kernel, layer),
        grid=(BATCH, A_HEADS, nt),
        in_specs=[blk(BLK_AQ), blk(BLK_AF), blk(BLK_AI), blk(BLK_AZ),
                  pl.BlockSpec((1, DEPTH, LANES), lambda b, hd, t: (hd, 0, 0)),
                  pl.BlockSpec((None, 1, 1, LANES), lambda b, hd, t: (layer, hd, 0, 0)),
                  pl.BlockSpec((tt, D_MODEL), lambda b, hd, t: (b * nt + t, 0)),
                  pl.BlockSpec((None, slab * LANES, D_MODEL), lambda b, hd, t: (layer, hd, 0))],
        out_specs=[pl.BlockSpec((1, tt, LANES), lambda b, hd, t: (hd, b * nt + t, 0)),
                   pl.BlockSpec((slab, tt, LANES), lambda b, hd, t: (hd, b * nt + t, 0))],
        out_shape=[jax.ShapeDtypeStruct((A_HEADS, TOKENS, LANES), BF16),
                   jax.ShapeDtypeStruct((GATE_BLOCKS, TOKENS, LANES), BF16)],
        scratch_shapes=[pltpu.VMEM((LANES, LANES), F32),
                        pltpu.VMEM((tt, LANES), F32),
                        pltpu.VMEM((tt, LANES), F32),
                        pltpu.VMEM((tt, A_CHUNK), F32)],
        compiler_params=_cparams(3),
        name="hgrn2",
    )(proj, proj, proj, proj, lbl, nw, h, wg_t)


def _causal_conv(xbuf, x, cw, cb, first):
    tt = x.shape[0]

    @pl.when(first)
    def _():
        xbuf[0:CONV_HALO, :] = jnp.zeros((CONV_HALO, LANES), F32)

    xbuf[CONV_HALO:CONV_HALO + tt, :] = x
    y = cb
    for k in range(B_CONV):
        y = y + cw[k:k + 1, :] * xbuf[pl.ds(CONV_HALO - (B_CONV - 1) + k, tt), :]
    xbuf[0:CONV_HALO, :] = x[tt - CONV_HALO:tt, :]
    return y


def _rglru_kernel(x_ref, z_ref, cw_ref, cb_ref, wa_ref, ba_ref, wx_ref, bx_ref, lam_ref, h_ref, wg_ref,
                  o_ref, g_ref, xbuf, a_s, u_s, hcar):
    tt = x_ref.shape[1]
    first = pl.program_id(1) == 0

    @pl.when(first)
    def _():
        hcar[...] = jnp.zeros_like(hcar)

    xcs, rs, igs = [], [], []
    for h in range(B_HEADS):
        xc = _causal_conv(xbuf.at[h], x_ref[h].astype(F32), cw_ref[h], cb_ref[h], first)
        xcb = xc.astype(BF16)
        xcs.append(xc)
        rs.append(jnp.dot(xcb, wa_ref[h], preferred_element_type=F32))
        igs.append(jnp.dot(xcb, wx_ref[h], preferred_element_type=F32))
    _store_gate_logits(g_ref, h_ref[...], wg_ref[...])

    rowm = lax.broadcasted_iota(jnp.int32, (tt, LANES), 0) & (SUBLANES - 1)
    for h in range(B_HEADS):
        xc = xcs[h]
        r = _sigmoid(rs[h] + ba_ref[h])
        ig = _sigmoid(igs[h] + bx_ref[h])
        log_a = (-B_C) * r * _softplus(-lam_ref[h])
        a = jnp.exp(log_a)
        u = jnp.sqrt(1.0 - jnp.exp(2.0 * log_a)) * (ig * xc)
        for s in (1, 2, 4):
            a_sh = pltpu.roll(a, s, 0)
            u_sh = pltpu.roll(u, s, 0)
            m = rowm >= s
            u = jnp.where(m, a * u_sh + u, u)
            a = jnp.where(m, a * a_sh, a)
        a_s[h] = a
        u_s[h] = u

    def grp(g, hs):
        rows = pl.ds(pl.multiple_of(g * SUBLANES, SUBLANES), SUBLANES)
        new = []
        for h in range(B_HEADS):
            hg = a_s[h, rows, :] * hs[h] + u_s[h, rows, :]
            u_s[h, rows, :] = hg
            new.append(jnp.broadcast_to(hg[SUBLANES - 1:SUBLANES, :], (SUBLANES, LANES)))
        return tuple(new)

    hs = lax.fori_loop(0, tt // SUBLANES, grp, tuple(hcar[h] for h in range(B_HEADS)), unroll=4)
    for h in range(B_HEADS):
        hcar[h] = hs[h]
        o_ref[h] = (u_s[h] * _silu(z_ref[h].astype(F32))).astype(o_ref.dtype)


def _rglru(proj, h, wg_t, cw, cb, wa, ba, wx, bx, lam, layer, tt=512):
    nt = SEQ // tt
    rows = lambda b, t: b * nt + t
    blk = lambda off: pl.BlockSpec((B_HEADS, tt, LANES), lambda b, t: (off // B_HEADS, rows(b, t), 0))
    par = lambda r, c: pl.BlockSpec((None, B_HEADS, r, c), lambda b, t: (layer, 0, 0, 0))
    return pl.pallas_call(
        _rglru_kernel,
        grid=(BATCH, nt),
        in_specs=[blk(BLK_BX), blk(BLK_BZ), par(B_CONV, LANES), par(1, LANES),
                  par(LANES, LANES), par(1, LANES), par(LANES, LANES), par(1, LANES), par(1, LANES),
                  pl.BlockSpec((tt, D_MODEL), lambda b, t: (rows(b, t), 0)),
                  pl.BlockSpec((None, D_MODEL, D_MODEL), lambda b, t: (layer, 1, 0))],
        out_specs=[pl.BlockSpec((B_HEADS, tt, LANES), lambda b, t: (0, rows(b, t), 0)),
                   pl.BlockSpec((GATE_BLOCKS, tt, LANES), lambda b, t: (0, rows(b, t), 0))],
        out_shape=[jax.ShapeDtypeStruct((B_HEADS, TOKENS, LANES), BF16),
                   jax.ShapeDtypeStruct((GATE_BLOCKS, TOKENS, LANES), BF16)],
        scratch_shapes=[pltpu.VMEM((B_HEADS, tt + CONV_HALO, LANES), F32),
                        pltpu.VMEM((B_HEADS, tt, LANES), F32),
                        pltpu.VMEM((B_HEADS, tt, LANES), F32),
                        pltpu.VMEM((B_HEADS, SUBLANES, LANES), F32)],
        compiler_params=_cparams(2),
        name="rglru",
    )(proj, proj, cw, cb, wa, ba, wx, bx, lam, h, wg_t)


def _ssd_kernel(z_ref, x_ref, bm_ref, cm_ref, dt_ref, cwx_ref, cwb_ref, cwc_ref, cbx_ref, cbb_ref, cbc_ref,
                dtb_ref, alog_ref, dsk_ref, nw_ref, h_ref, wg_ref, o_ref, g_ref, xbuf, st_ref):
    tt = x_ref.shape[1]
    nxb = C_XBLOCKS
    first = pl.program_id(1) == 0

    @pl.when(first)
    def _():
        st_ref[...] = jnp.zeros_like(st_ref)

    _store_gate_logits(g_ref, h_ref[...], wg_ref[...])

    dt = _softplus(dt_ref[0] + dtb_ref[...])
    da = dt * (-jnp.exp(alog_ref[...]))
    causal = _tril(tt)
    cum = _dot_exact_lhs(causal.astype(BF16), da)
    cum_t = cum.T
    dt_t = dt.T
    ecum = jnp.exp(cum)
    cl = cum[tt - 1:tt, :]
    w = dt * jnp.exp(cl - cum)
    ecl = jnp.broadcast_to(jnp.exp(cl), (SUBLANES, LANES))
    lane = lax.broadcasted_iota(jnp.int32, (tt, LANES), 1)
    er = lax.broadcasted_iota(jnp.int32, (2 * LANES, C_GROUP_W), 0) & (LANES - 1)
    ec = lax.broadcasted_iota(jnp.int32, (2 * LANES, C_GROUP_W), 1)

    for g in range(C_GROUPS):
        xs_blocks = [_silu(_causal_conv(xbuf.at[g * nxb + p], x_ref[g * nxb + p].astype(F32),
                                        cwx_ref[g * nxb + p], cbx_ref[g * nxb + p], first)) for p in range(nxb)]
        bm = _silu(_causal_conv(xbuf.at[2 * nxb + g], bm_ref[g].astype(F32), cwb_ref[g], cbb_ref[g], first))
        cm = _silu(_causal_conv(xbuf.at[2 * nxb + C_GROUPS + g], cm_ref[g].astype(F32), cwc_ref[g], cbc_ref[g], first))
        cmb = cm.astype(BF16)
        bmb = bm.astype(BF16)
        cb = lax.dot_general(cmb, bmb, _NT, preferred_element_type=F32)
        y_blocks = []
        for p in range(nxb):
            xb = xs_blocks[p]
            acc = None
            for half in range(2):
                hl = g * C_GROUP_HEADS + 2 * p + half
                seg = cum[:, hl:hl + 1] - cum_t[hl:hl + 1, :]
                lmat = jnp.exp(jnp.where(causal, seg, -1e30))
                m = (cb * lmat * dt_t[hl:hl + 1, :]).astype(BF16)
                in_head = (lane >= C_HDIM) if half else (lane < C_HDIM)
                rhs = jnp.where(in_head, xb, 0.0).astype(BF16)
                part = jnp.dot(m, rhs, preferred_element_type=F32)
                acc = part if acc is None else acc + part
            y_blocks.append(acc)
        xs = jnp.concatenate(xs_blocks, axis=1)
        y = jnp.concatenate(y_blocks, axis=1)
        ej = er - g * C_GROUP_HEADS
        spread = ((ec >= ej * C_HDIM) & (ec < (ej + 1) * C_HDIM)).astype(BF16)
        st = st_ref[g]
        y = y + jnp.dot(cmb, st.astype(BF16), preferred_element_type=F32) * _dot_hilo_rhs(ecum, spread)
        y = y + jnp.concatenate([dsk_ref[g * nxb + p] for p in range(nxb)], axis=1) * xs
        xw = (xs * _dot_hilo_rhs(w, spread)).astype(BF16)
        dec = _dot_hilo_rhs(ecl, spread)[0:1, :]
        st_ref[g] = st * dec + lax.dot_general(bmb, xw, _TN, preferred_element_type=F32)
        zf = jnp.concatenate([z_ref[g * nxb + p].astype(F32) for p in range(nxb)], axis=1)
        yg = y * _silu(zf)
        ms = jnp.mean(yg * yg, axis=-1, keepdims=True)
        out = yg * lax.rsqrt(ms + EPS) * jnp.concatenate([nw_ref[g * nxb + p] for p in range(nxb)], axis=1)
        for p in range(nxb):
            o_ref[g * nxb + p] = out[:, p * LANES:(p + 1) * LANES].astype(o_ref.dtype)


def _ssd(proj, dtp, h, wg_t, cw, cb, dtb, alog, dsk, nw, layer):
    tt = C_CHUNK
    nt = SEQ // tt
    nx = C_GROUPS * C_XBLOCKS
    rows = lambda b, t: b * nt + t
    big = lambda off: pl.BlockSpec((nx, tt, LANES), lambda b, t: (off // nx, rows(b, t), 0))
    two = lambda off: pl.BlockSpec((C_GROUPS, tt, LANES), lambda b, t: (off // C_GROUPS, rows(b, t), 0))
    par = lambda n, r, blk: pl.BlockSpec((None, n, r, LANES), lambda b, t: (layer, blk, 0, 0))
    vec = pl.BlockSpec((None, 1, LANES), lambda b, t: (layer, 0, 0))
    return pl.pallas_call(
        _ssd_kernel,
        grid=(BATCH, nt),
        in_specs=[big(BLK_CZ), big(BLK_CX), two(BLK_CB), two(BLK_CC),
                  pl.BlockSpec((1, tt, LANES), lambda b, t: (0, rows(b, t), 0)),
                  par(nx, C_CONV, 0), par(C_GROUPS, C_CONV, nx // C_GROUPS), par(C_GROUPS, C_CONV, nx // C_GROUPS + 1),
                  par(nx, 1, 0), par(C_GROUPS, 1, nx // C_GROUPS), par(C_GROUPS, 1, nx // C_GROUPS + 1),
                  vec, vec, par(nx, 1, 0), par(nx, 1, 0),
                  pl.BlockSpec((tt, D_MODEL), lambda b, t: (rows(b, t), 0)),
                  pl.BlockSpec((None, D_MODEL, D_MODEL), lambda b, t: (layer, 2, 0))],
        out_specs=[pl.BlockSpec((nx, tt, LANES), lambda b, t: (0, rows(b, t), 0)),
                   pl.BlockSpec((GATE_BLOCKS, tt, LANES), lambda b, t: (0, rows(b, t), 0))],
        out_shape=[jax.ShapeDtypeStruct((nx, TOKENS, LANES), BF16),
                   jax.ShapeDtypeStruct((GATE_BLOCKS, TOKENS, LANES), BF16)],
        scratch_shapes=[pltpu.VMEM((nx + 2 * C_GROUPS, tt + CONV_HALO, LANES), F32),
                        pltpu.VMEM((C_GROUPS, C_STATE, C_GROUP_W), F32)],
        compiler_params=_cparams(2),
        name="ssd",
    )(proj, proj, proj, proj, dtp, cw, cw, cw, cb, cb, cb, dtb, alog, dsk, nw, h, wg_t)


def _merge_kernel(final, oa_ref, ob_ref, oc_ref, ga_ref, gb_ref, gc_ref, x_ref, w_ref, nw_ref, *out_refs):
    nb = WIDTH // LANES
    acc = x_ref[...]
    for br, (o_ref, g_ref) in enumerate(((oa_ref, ga_ref), (ob_ref, gb_ref), (oc_ref, gc_ref))):
        o = jnp.concatenate([o_ref[hh] for hh in range(nb)], axis=1)
        bo = jnp.dot(o, w_ref[br], preferred_element_type=F32)
        gl = jnp.concatenate([g_ref[c] for c in range(GATE_BLOCKS)], axis=1).astype(F32)
        acc = acc + _sigmoid(gl) * bo
    ms = jnp.mean(acc * acc, axis=-1, keepdims=True)
    hn = acc * lax.rsqrt(ms + EPS) * nw_ref[...]
    if final:
        out_refs[0][...] = hn.astype(out_refs[0].dtype)
    else:
        out_refs[0][...] = acc
        out_refs[1][...] = hn.astype(out_refs[1].dtype)


def _merge(oa, ob, oc, ga, gb, gc, x, w_out, nw, layer, final, tm=256):
    nb = WIDTH // LANES
    obs = pl.BlockSpec((nb, tm, LANES), lambda i: (0, i, 0))
    gbs = pl.BlockSpec((GATE_BLOCKS, tm, LANES), lambda i: (0, i, 0))
    row = pl.BlockSpec((tm, D_MODEL), lambda i: (i, 0))
    if final:
        out_specs = row
        out_shape = jax.ShapeDtypeStruct((TOKENS, D_MODEL), F32)
    else:
        out_specs = [row, row]
        out_shape = [jax.ShapeDtypeStruct((TOKENS, D_MODEL), F32),
                     jax.ShapeDtypeStruct((TOKENS, D_MODEL), BF16)]
    return pl.pallas_call(
        functools.partial(_merge_kernel, final),
        grid=(TOKENS // tm,),
        in_specs=[obs, obs, obs, gbs, gbs, gbs, row,
                  pl.BlockSpec((None, N_BRANCH, WIDTH, D_MODEL), lambda i: (layer, 0, 0, 0)),
                  pl.BlockSpec((1, D_MODEL), lambda i: (0, 0))],
        out_specs=out_specs,
        out_shape=out_shape,
        compiler_params=_cparams(1),
        name="merge",
    )(oa, ob, oc, ga, gb, gc, x, w_out, nw)


def _blocks(p, rows):
    d, r, n = p.shape
    return p.reshape(d, r, n // LANES, LANES).transpose(0, 2, 1, 3)


def _pad_lanes(p):
    return jnp.pad(p, ((0, 0),) * (p.ndim - 1) + ((0, LANES - p.shape[-1]),))


def kernel(x, norm_w, w_in, hgrn_lb_logits, hgrn_norm_w, rglru_conv_w, rglru_conv_b, rglru_wa, rglru_ba,
           rglru_wx, rglru_bx, rglru_lambda, ssd_conv_w, ssd_conv_b, ssd_dt_bias, ssd_a_log, ssd_d,
           ssd_norm_w, w_out, final_norm_w):
    w_t = jnp.swapaxes(w_in, 1, 2)
    wg_t = w_t[:, GATE_COL0:, :].astype(BF16)
    w_o = w_out.reshape(DEPTH, N_BRANCH, WIDTH, D_MODEL).astype(BF16)
    lbl = hgrn_lb_logits.reshape(DEPTH, A_HEADS, LANES).transpose(1, 0, 2)
    a_nw = _blocks(hgrn_norm_w[:, None, :], 1)
    b_cw = _blocks(rglru_conv_w, B_CONV)
    b_cb = _blocks(rglru_conv_b[:, None, :], 1)
    b_wa = rglru_wa.astype(BF16)
    b_wx = rglru_wx.astype(BF16)
    b_ba = rglru_ba[:, :, None, :]
    b_bx = rglru_bx[:, :, None, :]
    b_lam = _blocks(rglru_lambda[:, None, :], 1)
    c_cw = _blocks(ssd_conv_w, C_CONV)
    c_cb = _blocks(ssd_conv_b[:, None, :], 1)
    c_dtb = _pad_lanes(ssd_dt_bias)[:, None, :]
    c_alog = _pad_lanes(ssd_a_log)[:, None, :]
    c_dsk = _blocks(jnp.repeat(ssd_d, C_HDIM, axis=1)[:, None, :], 1)
    c_nw = _blocks(ssd_norm_w[:, None, :], 1)

    xf = x.reshape(TOKENS, D_MODEL)
    h = _rmsnorm(xf, norm_w[0:1], BF16)
    for l in range(DEPTH):
        proj, dtp = _project(h, w_t, l)
        oa, ga = _hgrn2(proj, h, wg_t, lbl, a_nw, l)
        ob, gb = _rglru(proj, h, wg_t, b_cw, b_cb, b_wa, b_ba, b_wx, b_bx, b_lam, l)
        oc, gc = _ssd(proj, dtp, h, wg_t, c_cw, c_cb, c_dtb, c_alog, c_dsk, c_nw, l)
        if l + 1 < DEPTH:
            xf, h = _merge(oa, ob, oc, ga, gb, gc, xf, w_o, norm_w[l + 1:l + 2], l, final=False)
        else:
            out = _merge(oa, ob, oc, ga, gb, gc, xf, w_o, final_norm_w[None, :], l, final=True)
    return out.reshape(BATCH, SEQ, D_MODEL)
```

```python
import functools

import jax
import jax.numpy as jnp
from jax import lax
from jax.experimental import pallas as pl
from jax.experimental.pallas import tpu as pltpu

F32 = jnp.float32
BF16 = jnp.bfloat16

D_MODEL = 2048
BATCH = 2
SEQ = 4096
DEPTH = 4
TOKENS = BATCH * SEQ
EPS = 1e-6
WIDTH = 1024
N_BRANCH = 3
LANES = 128
SUBLANES = 8
A_HEADS = 8
A_CHUNK = 64
A_MAX_DECAY = 80.0
B_HEADS = 8
B_CONV = 4
B_C = 8.0
C_HDIM = 64
C_HEADS = 16
C_GROUPS = 2
C_STATE = 128
C_CONV = 4
C_CHUNK = 128
C_GROUP_W = WIDTH // C_GROUPS
C_GROUP_HEADS = C_HEADS // C_GROUPS
C_XBLOCKS = C_GROUP_W // LANES
MAIN_COLS = 8 * WIDTH + 2 * C_GROUPS * C_STATE
GATE_COL0 = MAIN_COLS + C_HEADS
PROJ_TN = 512
PROJ_TILES = MAIN_COLS // PROJ_TN
TILE_BLOCKS = PROJ_TN // LANES
BLK_AQ, BLK_AF, BLK_AI, BLK_AZ = 0, 8, 16, 24
BLK_BX, BLK_BZ = 32, 40
BLK_CZ, BLK_CX, BLK_CB, BLK_CC = 48, 56, 64, 66
MAIN_BLOCKS = MAIN_COLS // LANES
GATE_BLOCKS = D_MODEL // LANES
CONV_HALO = SUBLANES
VMEM_LIMIT = 56 * 1024 * 1024

_NT = (((1,), (1,)), ((), ()))
_TN = (((0,), (0,)), ((), ()))


def _cparams(n_axes):
    return pltpu.CompilerParams(dimension_semantics=("arbitrary",) * n_axes,
                                vmem_limit_bytes=VMEM_LIMIT)


def _sigmoid(x):
    return 1.0 / (1.0 + jnp.exp(-x))


def _silu(x):
    return x * _sigmoid(x)


def _softplus(x):
    return jnp.maximum(x, 0.0) + jnp.log1p(jnp.exp(-jnp.abs(x)))


def _dot_exact_lhs(m01, x):
    x1 = x.astype(BF16)
    r1 = x - x1.astype(F32)
    x2 = r1.astype(BF16)
    x3 = (r1 - x2.astype(F32)).astype(BF16)
    d = lambda a: jnp.dot(m01, a, preferred_element_type=F32)
    return d(x1) + d(x2) + d(x3)


def _dot_hilo_rhs(x, m01_twice):
    x1 = x.astype(BF16)
    x2 = (x - x1.astype(F32)).astype(BF16)
    return jnp.dot(jnp.concatenate([x1, x2], axis=1), m01_twice, preferred_element_type=F32)


def _tril(n):
    r = lax.broadcasted_iota(jnp.int32, (n, n), 0)
    c = lax.broadcasted_iota(jnp.int32, (n, n), 1)
    return r >= c


def _rms_kernel(x_ref, w_ref, o_ref):
    x = x_ref[...]
    ms = jnp.mean(x * x, axis=-1, keepdims=True)
    o_ref[...] = (x * lax.rsqrt(ms + EPS) * w_ref[...]).astype(o_ref.dtype)


def _rmsnorm(x, w, out_dtype, tm=512):
    t, d = x.shape
    return pl.pallas_call(
        _rms_kernel,
        grid=(t // tm,),
        in_specs=[pl.BlockSpec((tm, d), lambda i: (i, 0)),
                  pl.BlockSpec((1, d), lambda i: (0, 0))],
        out_specs=pl.BlockSpec((tm, d), lambda i: (i, 0)),
        out_shape=jax.ShapeDtypeStruct((t, d), out_dtype),
        compiler_params=_cparams(1),
        name="rmsnorm",
    )(x, w)


def _proj_kernel(h_ref, w_ref, wdt_ref, o_ref, dt_ref):
    h = h_ref[...]
    acc = lax.dot_general(h, w_ref[...].astype(BF16), _NT, preferred_element_type=F32)
    for c in range(TILE_BLOCKS):
        o_ref[c] = acc[:, c * LANES:(c + 1) * LANES].astype(o_ref.dtype)

    @pl.when(pl.program_id(1) == PROJ_TILES - 1)
    def _():
        dt_ref[0] = lax.dot_general(h, wdt_ref[...].astype(BF16), _NT, preferred_element_type=F32)


def _project(h, w_t, layer, tm=2048):
    t, d = h.shape
    return pl.pallas_call(
        _proj_kernel,
        grid=(t // tm, PROJ_TILES),
        in_specs=[pl.BlockSpec((tm, d), lambda i, j: (i, 0)),
                  pl.BlockSpec((None, PROJ_TN, d), lambda i, j: (layer, j, 0)),
                  pl.BlockSpec((None, LANES, d), lambda i, j: (layer, MAIN_COLS // LANES, 0))],
        out_specs=[pl.BlockSpec((TILE_BLOCKS, tm, LANES), lambda i, j: (j, i, 0)),
                   pl.BlockSpec((1, tm, LANES), lambda i, j: (0, i, 0))],
        out_shape=[jax.ShapeDtypeStruct((MAIN_BLOCKS, t, LANES), BF16),
                   jax.ShapeDtypeStruct((1, t, LANES), F32)],
        compiler_params=_cparams(2),
        name="proj_in",
    )(h, w_t, w_t)


def _gate_kernel(h_ref, w_ref, o_ref):
    acc = lax.dot_general(h_ref[...], w_ref[...], _NT, preferred_element_type=F32)
    for c in range(TILE_BLOCKS):
        o_ref[c] = acc[:, c * LANES:(c + 1) * LANES].astype(o_ref.dtype)


def _project_gates(h, wg_t, layer, tm=2048):
    t, d = h.shape
    n = wg_t.shape[1]
    return pl.pallas_call(
        _gate_kernel,
        grid=(t // tm, n // PROJ_TN),
        in_specs=[pl.BlockSpec((tm, d), lambda i, j: (i, 0)),
                  pl.BlockSpec((None, PROJ_TN, d), lambda i, j: (layer, j, 0))],
        out_specs=pl.BlockSpec((TILE_BLOCKS, tm, LANES), lambda i, j: (j, i, 0)),
        out_shape=jax.ShapeDtypeStruct((n // LANES, t, LANES), BF16),
        compiler_params=_cparams(2),
        name="proj_gate",
    )(h, wg_t)


def _hgrn_kernel(layer, q_ref, f_ref, i_ref, z_ref, lbl_ref, nw_ref, o_ref, st_ref, b_ref, k_ref, a_ref):
    tt = q_ref.shape[1]

    @pl.when(pl.program_id(2) == 0)
    def _():
        st_ref[...] = jnp.zeros_like(st_ref)

    lbl = lbl_ref[0]
    e = jnp.exp(lbl - jnp.max(lbl, axis=0, keepdims=True))
    lb = jnp.zeros((1, LANES), F32)
    for j in range(1, layer + 1):
        lb = lb + e[j:j + 1, :]
    lb = lb / jnp.sum(e, axis=0, keepdims=True)
    nw = nw_ref[0]

    nc = tt // A_CHUNK
    chunks = [slice(c * A_CHUNK, (c + 1) * A_CHUNK) for c in range(nc)]
    fx = f_ref[0].astype(F32)
    f = lb + (1.0 - lb) * _sigmoid(fx)
    g = jnp.maximum(jnp.log(f), -128.0)
    kk = 1.0 - f
    rc = lax.broadcasted_iota(jnp.int32, (tt, LANES), 0) & (A_CHUNK - 1)
    b = g
    s = 1
    while s < A_CHUNK:
        b = b + jnp.where(rc >= s, pltpu.roll(b, s, 0), 0.0)
        s *= 2
    q = q_ref[0].astype(F32)
    qs = (q * jnp.exp(b)).astype(BF16)
    b_ref[...] = b
    k_ref[...] = kk
    causal = _tril(A_CHUNK)
    small_decay = jnp.min(b) >= -A_MAX_DECAY

    @pl.when(small_decay)
    def _():
        kt = (kk * jnp.exp(-b)).astype(BF16)
        for rows in chunks:
            a = lax.dot_general(qs[rows], kt[rows], _NT, preferred_element_type=F32)
            a_ref[rows, :] = jnp.where(causal, a, 0.0)

    @pl.when(jnp.logical_not(small_decay))
    def _():
        coli = lax.broadcasted_iota(jnp.int32, (A_CHUNK, A_CHUNK), 1)

        def chunk_body(c, carry):
            r0 = pl.multiple_of(c * A_CHUNK, A_CHUNK)
            rows = pl.ds(r0, A_CHUNK)
            bc = b_ref[rows, :]
            qc = q_ref[0, rows, :].astype(F32)

            def col_body(j, acc):
                bj = b_ref[pl.ds(r0 + j, 1), :]
                kj = k_ref[pl.ds(r0 + j, 1), :]
                p = qc * jnp.exp(jnp.minimum(bc - bj, 0.0)) * kj
                return jnp.where(coli == j, jnp.sum(p, axis=-1, keepdims=True), acc)

            acc = lax.fori_loop(0, A_CHUNK, col_body, jnp.zeros((A_CHUNK, A_CHUNK), F32))
            a_ref[rows, :] = jnp.where(causal, acc, 0.0)
            return carry

        lax.fori_loop(0, nc, chunk_body, 0)

    v = i_ref[0]
    b_last = [b[r.stop - 1:r.stop, :] for r in chunks]
    bl_full = jnp.concatenate([jnp.broadcast_to(bl, (A_CHUNK, LANES)) for bl in b_last], axis=0)
    kd = (kk * jnp.exp(bl_full - b)).astype(BF16)
    o_intra = [jnp.dot(a_ref[r, :].astype(BF16), v[r], preferred_element_type=F32) for r in chunks]
    kv = [lax.dot_general(v[r], kd[r], _TN, preferred_element_type=F32) for r in chunks]
    st = st_ref[...]
    outs = []
    for c, r in enumerate(chunks):
        outs.append(o_intra[c] + lax.dot_general(qs[r], st.astype(BF16), _NT, preferred_element_type=F32))
        st = st * jnp.exp(b_last[c]) + kv[c]
    st_ref[...] = st
    o = jnp.concatenate(outs, axis=0)
    ms = jnp.mean(o * o, axis=-1, keepdims=True)
    y = o * lax.rsqrt(ms + EPS) * nw
    o_ref[0] = (y * _silu(z_ref[0].astype(F32))).astype(o_ref.dtype)


def _hgrn2(proj, lbl, nw, layer, tt=1024):
    nt = SEQ // tt
    blk = lambda off: pl.BlockSpec((1, tt, LANES), lambda b, hd, t: (off + hd, b * nt + t, 0))
    return pl.pallas_call(
        functools.partial(_hgrn_kernel, layer),
        grid=(BATCH, A_HEADS, nt),
        in_specs=[blk(BLK_AQ), blk(BLK_AF), blk(BLK_AI), blk(BLK_AZ),
                  pl.BlockSpec((1, DEPTH, LANES), lambda b, hd, t: (hd, 0, 0)),
                  pl.BlockSpec((None, 1, 1, LANES), lambda b, hd, t: (layer, hd, 0, 0))],
        out_specs=pl.BlockSpec((1, tt, LANES), lambda b, hd, t: (hd, b * nt + t, 0)),
        out_shape=jax.ShapeDtypeStruct((A_HEADS, TOKENS, LANES), BF16),
        scratch_shapes=[pltpu.VMEM((LANES, LANES), F32),
                        pltpu.VMEM((tt, LANES), F32),
                        pltpu.VMEM((tt, LANES), F32),
                        pltpu.VMEM((tt, A_CHUNK), F32)],
        compiler_params=_cparams(3),
        name="hgrn2",
    )(proj, proj, proj, proj, lbl, nw)


def _causal_conv(xbuf, x, cw, cb, first):
    tt = x.shape[0]

    @pl.when(first)
    def _():
        xbuf[0:CONV_HALO, :] = jnp.zeros((CONV_HALO, LANES), F32)

    xbuf[CONV_HALO:CONV_HALO + tt, :] = x
    y = cb
    for k in range(B_CONV):
        y = y + cw[k:k + 1, :] * xbuf[pl.ds(CONV_HALO - (B_CONV - 1) + k, tt), :]
    xbuf[0:CONV_HALO, :] = x[tt - CONV_HALO:tt, :]
    return y


def _rglru_kernel(x_ref, z_ref, cw_ref, cb_ref, wa_ref, ba_ref, wx_ref, bx_ref, lam_ref, o_ref,
                  xbuf, a_s, u_s, hcar):
    tt = x_ref.shape[1]
    first = pl.program_id(1) == 0

    @pl.when(first)
    def _():
        hcar[...] = jnp.zeros_like(hcar)

    rowm = lax.broadcasted_iota(jnp.int32, (tt, LANES), 0) & (SUBLANES - 1)
    for h in range(B_HEADS):
        xc = _causal_conv(xbuf.at[h], x_ref[h].astype(F32), cw_ref[h], cb_ref[h], first)
        xcb = xc.astype(BF16)
        r = _sigmoid(jnp.dot(xcb, wa_ref[h], preferred_element_type=F32) + ba_ref[h])
        ig = _sigmoid(jnp.dot(xcb, wx_ref[h], preferred_element_type=F32) + bx_ref[h])
        log_a = (-B_C) * r * _softplus(-lam_ref[h])
        a = jnp.exp(log_a)
        u = jnp.sqrt(1.0 - jnp.exp(2.0 * log_a)) * (ig * xc)
        for s in (1, 2, 4):
            a_sh = pltpu.roll(a, s, 0)
            u_sh = pltpu.roll(u, s, 0)
            m = rowm >= s
            u = jnp.where(m, a * u_sh + u, u)
            a = jnp.where(m, a * a_sh, a)
        a_s[h] = a
        u_s[h] = u

    def grp(g, hs):
        rows = pl.ds(pl.multiple_of(g * SUBLANES, SUBLANES), SUBLANES)
        new = []
        for h in range(B_HEADS):
            hg = a_s[h, rows, :] * hs[h] + u_s[h, rows, :]
            u_s[h, rows, :] = hg
            new.append(jnp.broadcast_to(hg[SUBLANES - 1:SUBLANES, :], (SUBLANES, LANES)))
        return tuple(new)

    hs = lax.fori_loop(0, tt // SUBLANES, grp, tuple(hcar[h] for h in range(B_HEADS)), unroll=4)
    for h in range(B_HEADS):
        hcar[h] = hs[h]
        o_ref[h] = (u_s[h] * _silu(z_ref[h].astype(F32))).astype(o_ref.dtype)


def _rglru(proj, cw, cb, wa, ba, wx, bx, lam, layer, tt=512):
    nt = SEQ // tt
    rows = lambda b, t: b * nt + t
    blk = lambda off: pl.BlockSpec((B_HEADS, tt, LANES), lambda b, t: (off // B_HEADS, rows(b, t), 0))
    par = lambda r, c: pl.BlockSpec((None, B_HEADS, r, c), lambda b, t: (layer, 0, 0, 0))
    return pl.pallas_call(
        _rglru_kernel,
        grid=(BATCH, nt),
        in_specs=[blk(BLK_BX), blk(BLK_BZ), par(B_CONV, LANES), par(1, LANES),
                  par(LANES, LANES), par(1, LANES), par(LANES, LANES), par(1, LANES), par(1, LANES)],
        out_specs=pl.BlockSpec((B_HEADS, tt, LANES), lambda b, t: (0, rows(b, t), 0)),
        out_shape=jax.ShapeDtypeStruct((B_HEADS, TOKENS, LANES), BF16),
        scratch_shapes=[pltpu.VMEM((B_HEADS, tt + CONV_HALO, LANES), F32),
                        pltpu.VMEM((B_HEADS, tt, LANES), F32),
                        pltpu.VMEM((B_HEADS, tt, LANES), F32),
                        pltpu.VMEM((B_HEADS, SUBLANES, LANES), F32)],
        compiler_params=_cparams(2),
        name="rglru",
    )(proj, proj, cw, cb, wa, ba, wx, bx, lam)


def _ssd_kernel(z_ref, x_ref, bm_ref, cm_ref, dt_ref, cwx_ref, cwb_ref, cwc_ref, cbx_ref, cbb_ref, cbc_ref,
                dtb_ref, alog_ref, dsk_ref, nw_ref, o_ref, xbuf, st_ref):
    tt = x_ref.shape[1]
    nxb = C_XBLOCKS
    first = pl.program_id(1) == 0

    @pl.when(first)
    def _():
        st_ref[...] = jnp.zeros_like(st_ref)

    dt_all = _softplus(dt_ref[0] + dtb_ref[...])
    da_all = dt_all * (-jnp.exp(alog_ref[...]))
    causal = _tril(C_CHUNK)
    causal_b = causal.astype(BF16)
    lane = lax.broadcasted_iota(jnp.int32, (C_CHUNK, LANES), 1)
    er = lax.broadcasted_iota(jnp.int32, (2 * LANES, C_GROUP_W), 0) & (LANES - 1)
    ec = lax.broadcasted_iota(jnp.int32, (2 * LANES, C_GROUP_W), 1)

    xs_all, bm_all, cm_all, spreads, dsk, nw, st = [], [], [], [], [], [], []
    for g in range(C_GROUPS):
        xs_all.append([_silu(_causal_conv(xbuf.at[g * nxb + p], x_ref[g * nxb + p].astype(F32),
                                          cwx_ref[g * nxb + p], cbx_ref[g * nxb + p], first)) for p in range(nxb)])
        bm_all.append(_silu(_causal_conv(xbuf.at[2 * nxb + g], bm_ref[g].astype(F32), cwb_ref[g], cbb_ref[g],
                                         first)).astype(BF16))
        cm_all.append(_silu(_causal_conv(xbuf.at[2 * nxb + C_GROUPS + g], cm_ref[g].astype(F32), cwc_ref[g],
                                         cbc_ref[g], first)).astype(BF16))
        ej = er - g * C_GROUP_HEADS
        spreads.append(((ec >= ej * C_HDIM) & (ec < (ej + 1) * C_HDIM)).astype(BF16))
        dsk.append(jnp.concatenate([dsk_ref[g * nxb + p] for p in range(nxb)], axis=1))
        nw.append(jnp.concatenate([nw_ref[g * nxb + p] for p in range(nxb)], axis=1))
        st.append(st_ref[g])

    for c in range(tt // C_CHUNK):
        rows = slice(c * C_CHUNK, (c + 1) * C_CHUNK)
        dt = dt_all[rows]
        cum = _dot_exact_lhs(causal_b, da_all[rows])
        cum_t = cum.T
        dt_t = dt.T
        ecum = jnp.exp(cum)
        cl = cum[C_CHUNK - 1:C_CHUNK, :]
        w = dt * jnp.exp(cl - cum)
        ecl = jnp.broadcast_to(jnp.exp(cl), (SUBLANES, LANES))
        for g in range(C_GROUPS):
            cmb = cm_all[g][rows]
            bmb = bm_all[g][rows]
            cb = lax.dot_general(cmb, bmb, _NT, preferred_element_type=F32)
            y_blocks = []
            for p in range(nxb):
                xb = xs_all[g][p][rows]
                acc = None
                for half in range(2):
                    hl = g * C_GROUP_HEADS + 2 * p + half
                    seg = cum[:, hl:hl + 1] - cum_t[hl:hl + 1, :]
                    lmat = jnp.exp(jnp.where(causal, seg, -1e30))
                    m = (cb * lmat * dt_t[hl:hl + 1, :]).astype(BF16)
                    in_head = (lane >= C_HDIM) if half else (lane < C_HDIM)
                    rhs = jnp.where(in_head, xb, 0.0).astype(BF16)
                    part = jnp.dot(m, rhs, preferred_element_type=F32)
                    acc = part if acc is None else acc + part
                y_blocks.append(acc)
            xs = jnp.concatenate([xs_all[g][p][rows] for p in range(nxb)], axis=1)
            y = jnp.concatenate(y_blocks, axis=1)
            y = y + (jnp.dot(cmb, st[g].astype(BF16), preferred_element_type=F32)
                     * _dot_hilo_rhs(ecum, spreads[g]))
            y = y + dsk[g] * xs
            xw = (xs * _dot_hilo_rhs(w, spreads[g])).astype(BF16)
            dec = _dot_hilo_rhs(ecl, spreads[g])[0:1, :]
            st[g] = st[g] * dec + lax.dot_general(bmb, xw, _TN, preferred_element_type=F32)
            zf = jnp.concatenate([z_ref[g * nxb + p, rows, :].astype(F32) for p in range(nxb)], axis=1)
            yg = y * _silu(zf)
            ms = jnp.mean(yg * yg, axis=-1, keepdims=True)
            out = yg * lax.rsqrt(ms + EPS) * nw[g]
            for p in range(nxb):
                o_ref[g * nxb + p, rows, :] = out[:, p * LANES:(p + 1) * LANES].astype(o_ref.dtype)

    for g in range(C_GROUPS):
        st_ref[g] = st[g]


def _ssd(proj, dtp, cw, cb, dtb, alog, dsk, nw, layer, tt=4 * C_CHUNK):
    nt = SEQ // tt
    nx = C_GROUPS * C_XBLOCKS
    rows = lambda b, t: b * nt + t
    big = lambda off: pl.BlockSpec((nx, tt, LANES), lambda b, t: (off // nx, rows(b, t), 0))
    two = lambda off: pl.BlockSpec((C_GROUPS, tt, LANES), lambda b, t: (off // C_GROUPS, rows(b, t), 0))
    par = lambda n, r, blk: pl.BlockSpec((None, n, r, LANES), lambda b, t: (layer, blk, 0, 0))
    vec = pl.BlockSpec((None, 1, LANES), lambda b, t: (layer, 0, 0))
    return pl.pallas_call(
        _ssd_kernel,
        grid=(BATCH, nt),
        in_specs=[big(BLK_CZ), big(BLK_CX), two(BLK_CB), two(BLK_CC),
                  pl.BlockSpec((1, tt, LANES), lambda b, t: (0, rows(b, t), 0)),
                  par(nx, C_CONV, 0), par(C_GROUPS, C_CONV, nx // C_GROUPS), par(C_GROUPS, C_CONV, nx // C_GROUPS + 1),
                  par(nx, 1, 0), par(C_GROUPS, 1, nx // C_GROUPS), par(C_GROUPS, 1, nx // C_GROUPS + 1),
                  vec, vec, par(nx, 1, 0), par(nx, 1, 0)],
        out_specs=pl.BlockSpec((nx, tt, LANES), lambda b, t: (0, rows(b, t), 0)),
        out_shape=jax.ShapeDtypeStruct((nx, TOKENS, LANES), BF16),
        scratch_shapes=[pltpu.VMEM((nx + 2 * C_GROUPS, tt + CONV_HALO, LANES), F32),
                        pltpu.VMEM((C_GROUPS, C_STATE, C_GROUP_W), F32)],
        compiler_params=_cparams(2),
        name="ssd",
    )(proj, proj, proj, proj, dtp, cw, cw, cw, cb, cb, cb, dtb, alog, dsk, nw)


def _merge_kernel(final, oa_ref, ob_ref, oc_ref, g_ref, x_ref, w_ref, nw_ref, *out_refs):
    nb = WIDTH // LANES
    acc = x_ref[...]
    for br, o_ref in enumerate((oa_ref, ob_ref, oc_ref)):
        o = jnp.concatenate([o_ref[hh] for hh in range(nb)], axis=1)
        bo = jnp.dot(o, w_ref[br], preferred_element_type=F32)
        gl = jnp.concatenate([g_ref[br * GATE_BLOCKS + c] for c in range(GATE_BLOCKS)], axis=1).astype(F32)
        acc = acc + _sigmoid(gl) * bo
    ms = jnp.mean(acc * acc, axis=-1, keepdims=True)
    hn = acc * lax.rsqrt(ms + EPS) * nw_ref[...]
    if final:
        out_refs[0][...] = hn.astype(out_refs[0].dtype)
    else:
        out_refs[0][...] = acc
        out_refs[1][...] = hn.astype(out_refs[1].dtype)


def _merge(oa, ob, oc, gates, x, w_out, nw, layer, final, tm=256):
    nb = WIDTH // LANES
    obs = pl.BlockSpec((nb, tm, LANES), lambda i: (0, i, 0))
    gbs = pl.BlockSpec((N_BRANCH * GATE_BLOCKS, tm, LANES), lambda i: (0, i, 0))
    row = pl.BlockSpec((tm, D_MODEL), lambda i: (i, 0))
    if final:
        out_specs = row
        out_shape = jax.ShapeDtypeStruct((TOKENS, D_MODEL), F32)
    else:
        out_specs = [row, row]
        out_shape = [jax.ShapeDtypeStruct((TOKENS, D_MODEL), F32),
                     jax.ShapeDtypeStruct((TOKENS, D_MODEL), BF16)]
    return pl.pallas_call(
        functools.partial(_merge_kernel, final),
        grid=(TOKENS // tm,),
        in_specs=[obs, obs, obs, gbs, row,
                  pl.BlockSpec((None, N_BRANCH, WIDTH, D_MODEL), lambda i: (layer, 0, 0, 0)),
                  pl.BlockSpec((1, D_MODEL), lambda i: (0, 0))],
        out_specs=out_specs,
        out_shape=out_shape,
        compiler_params=_cparams(1),
        name="merge",
    )(oa, ob, oc, gates, x, w_out, nw)


def _blocks(p, rows):
    d, r, n = p.shape
    return p.reshape(d, r, n // LANES, LANES).transpose(0, 2, 1, 3)


def _pad_lanes(p):
    return jnp.pad(p, ((0, 0),) * (p.ndim - 1) + ((0, LANES - p.shape[-1]),))


def kernel(x, norm_w, w_in, hgrn_lb_logits, hgrn_norm_w, rglru_conv_w, rglru_conv_b, rglru_wa, rglru_ba,
           rglru_wx, rglru_bx, rglru_lambda, ssd_conv_w, ssd_conv_b, ssd_dt_bias, ssd_a_log, ssd_d,
           ssd_norm_w, w_out, final_norm_w):
    w_t = jnp.swapaxes(w_in, 1, 2)
    wg_t = w_t[:, GATE_COL0:, :].astype(BF16)
    w_o = w_out.reshape(DEPTH, N_BRANCH, WIDTH, D_MODEL).astype(BF16)
    lbl = hgrn_lb_logits.reshape(DEPTH, A_HEADS, LANES).transpose(1, 0, 2)
    a_nw = _blocks(hgrn_norm_w[:, None, :], 1)
    b_cw = _blocks(rglru_conv_w, B_CONV)
    b_cb = _blocks(rglru_conv_b[:, None, :], 1)
    b_wa = rglru_wa.astype(BF16)
    b_wx = rglru_wx.astype(BF16)
    b_ba = rglru_ba[:, :, None, :]
    b_bx = rglru_bx[:, :, None, :]
    b_lam = _blocks(rglru_lambda[:, None, :], 1)
    c_cw = _blocks(ssd_conv_w, C_CONV)
    c_cb = _blocks(ssd_conv_b[:, None, :], 1)
    c_dtb = _pad_lanes(ssd_dt_bias)[:, None, :]
    c_alog = _pad_lanes(ssd_a_log)[:, None, :]
    c_dsk = _blocks(jnp.repeat(ssd_d, C_HDIM, axis=1)[:, None, :], 1)
    c_nw = _blocks(ssd_norm_w[:, None, :], 1)

    xf = x.reshape(TOKENS, D_MODEL)
    h = _rmsnorm(xf, norm_w[0:1], BF16)
    for l in range(DEPTH):
        proj, dtp = _project(h, w_t, l)
        gates = _project_gates(h, wg_t, l)
        oa = _hgrn2(proj, lbl, a_nw, l)
        ob = _rglru(proj, b_cw, b_cb, b_wa, b_ba, b_wx, b_bx, b_lam, l)
        oc = _ssd(proj, dtp, c_cw, c_cb, c_dtb, c_alog, c_dsk, c_nw, l)
        if l + 1 < DEPTH:
            xf, h = _merge(oa, ob, oc, gates, xf, w_o, norm_w[l + 1:l + 2], l, final=False)
        else:
            out = _merge(oa, ob, oc, gates, xf, w_o, final_norm_w[None, :], l, final=True)
    return out.reshape(BATCH, SEQ, D_MODEL)
```

```python
import functools

import jax
import jax.numpy as jnp
from jax import lax
from jax.experimental import pallas as pl
from jax.experimental.pallas import tpu as pltpu

F32 = jnp.float32
BF16 = jnp.bfloat16

D_MODEL = 2048
BATCH = 2
SEQ = 4096
DEPTH = 4
TOKENS = BATCH * SEQ
EPS = 1e-6
WIDTH = 1024
N_BRANCH = 3
LANES = 128
SUBLANES = 8
A_HEADS = 8
A_CHUNK = 64
A_MAX_DECAY = 80.0
B_HEADS = 8
B_CONV = 4
B_C = 8.0
C_HDIM = 64
C_HEADS = 16
C_GROUPS = 2
C_STATE = 128
C_CONV = 4
C_CHUNK = 128
C_GROUP_W = WIDTH // C_GROUPS
C_GROUP_HEADS = C_HEADS // C_GROUPS
C_XBLOCKS = C_GROUP_W // LANES
MAIN_COLS = 8 * WIDTH + 2 * C_GROUPS * C_STATE
GATE_COL0 = MAIN_COLS + C_HEADS
PROJ_TN = 512
PROJ_TILES = MAIN_COLS // PROJ_TN
TILE_BLOCKS = PROJ_TN // LANES
BLK_AQ, BLK_AF, BLK_AI, BLK_AZ = 0, 8, 16, 24
BLK_BX, BLK_BZ = 32, 40
BLK_CZ, BLK_CX, BLK_CB, BLK_CC = 48, 56, 64, 66
MAIN_BLOCKS = MAIN_COLS // LANES
GATE_BLOCKS = D_MODEL // LANES
CONV_HALO = SUBLANES
VMEM_LIMIT = 56 * 1024 * 1024

NEG_LOG2E = -1.4426950408889634
TINY = 1e-30
_NT = (((1,), (1,)), ((), ()))
_TN = (((0,), (0,)), ((), ()))


def _cparams(n_axes):
    return pltpu.CompilerParams(dimension_semantics=("arbitrary",) * n_axes,
                                vmem_limit_bytes=VMEM_LIMIT)


def _sigmoid(x):
    return 1.0 / (1.0 + jnp.exp2(x * NEG_LOG2E))


def _silu(x):
    return x * _sigmoid(x)


def _softplus(x):
    return jnp.maximum(x, 0.0) + jnp.log1p(jnp.exp(-jnp.abs(x)))


def _dot_exact_lhs(m01, x):
    x1 = x.astype(BF16)
    r1 = x - x1.astype(F32)
    x2 = r1.astype(BF16)
    x3 = (r1 - x2.astype(F32)).astype(BF16)
    d = lambda a: jnp.dot(m01, a, preferred_element_type=F32)
    return d(x1) + d(x2) + d(x3)


def _dot_hilo_rhs(x, m01_twice):
    x1 = x.astype(BF16)
    x2 = (x - x1.astype(F32)).astype(BF16)
    return jnp.dot(jnp.concatenate([x1, x2], axis=1), m01_twice, preferred_element_type=F32)


def _tril(n):
    r = lax.broadcasted_iota(jnp.int32, (n, n), 0)
    c = lax.broadcasted_iota(jnp.int32, (n, n), 1)
    return r >= c


def _rms_kernel(x_ref, w_ref, o_ref):
    x = x_ref[...]
    ms = jnp.mean(x * x, axis=-1, keepdims=True)
    o_ref[...] = (x * lax.rsqrt(ms + EPS) * w_ref[...]).astype(o_ref.dtype)


def _rmsnorm(x, w, out_dtype, tm=512):
    t, d = x.shape
    return pl.pallas_call(
        _rms_kernel,
        grid=(t // tm,),
        in_specs=[pl.BlockSpec((tm, d), lambda i: (i, 0)),
                  pl.BlockSpec((1, d), lambda i: (0, 0))],
        out_specs=pl.BlockSpec((tm, d), lambda i: (i, 0)),
        out_shape=jax.ShapeDtypeStruct((t, d), out_dtype),
        compiler_params=_cparams(1),
        name="rmsnorm",
    )(x, w)


def _proj_kernel(h_ref, w_ref, wdt_ref, o_ref, dt_ref):
    h = h_ref[...]
    acc = lax.dot_general(h, w_ref[...].astype(BF16), _NT, preferred_element_type=F32)
    for c in range(TILE_BLOCKS):
        o_ref[c] = acc[:, c * LANES:(c + 1) * LANES].astype(o_ref.dtype)

    @pl.when(pl.program_id(1) == PROJ_TILES - 1)
    def _():
        dt_ref[0] = lax.dot_general(h, wdt_ref[...].astype(BF16), _NT, preferred_element_type=F32)


def _project(h, w_t, layer, tm=2048):
    t, d = h.shape
    return pl.pallas_call(
        _proj_kernel,
        grid=(t // tm, PROJ_TILES),
        in_specs=[pl.BlockSpec((tm, d), lambda i, j: (i, 0)),
                  pl.BlockSpec((None, PROJ_TN, d), lambda i, j: (layer, j, 0)),
                  pl.BlockSpec((None, LANES, d), lambda i, j: (layer, MAIN_COLS // LANES, 0))],
        out_specs=[pl.BlockSpec((TILE_BLOCKS, tm, LANES), lambda i, j: (j, i, 0)),
                   pl.BlockSpec((1, tm, LANES), lambda i, j: (0, i, 0))],
        out_shape=[jax.ShapeDtypeStruct((MAIN_BLOCKS, t, LANES), BF16),
                   jax.ShapeDtypeStruct((1, t, LANES), F32)],
        compiler_params=_cparams(2),
        name="proj_in",
    )(h, w_t, w_t)


def _gate_kernel(h_ref, w_ref, o_ref):
    acc = lax.dot_general(h_ref[...], w_ref[...], _NT, preferred_element_type=F32)
    for c in range(TILE_BLOCKS):
        o_ref[c] = acc[:, c * LANES:(c + 1) * LANES].astype(o_ref.dtype)


def _project_gates(h, wg_t, layer, tm=2048):
    t, d = h.shape
    n = wg_t.shape[1]
    return pl.pallas_call(
        _gate_kernel,
        grid=(t // tm, n // PROJ_TN),
        in_specs=[pl.BlockSpec((tm, d), lambda i, j: (i, 0)),
                  pl.BlockSpec((None, PROJ_TN, d), lambda i, j: (layer, j, 0))],
        out_specs=pl.BlockSpec((TILE_BLOCKS, tm, LANES), lambda i, j: (j, i, 0)),
        out_shape=jax.ShapeDtypeStruct((n // LANES, t, LANES), BF16),
        compiler_params=_cparams(2),
        name="proj_gate",
    )(h, wg_t)


def _hgrn_kernel(layer, q_ref, f_ref, i_ref, z_ref, lbl_ref, nw_ref, o_ref, st_ref, b_ref, k_ref, a_ref):
    tt = q_ref.shape[1]

    @pl.when(pl.program_id(2) == 0)
    def _():
        st_ref[...] = jnp.zeros_like(st_ref)

    lbl = lbl_ref[0]
    e = jnp.exp(lbl - jnp.max(lbl, axis=0, keepdims=True))
    lb = jnp.zeros((1, LANES), F32)
    for j in range(1, layer + 1):
        lb = lb + e[j:j + 1, :]
    lb = lb / jnp.sum(e, axis=0, keepdims=True)
    nw = nw_ref[0]

    nc = tt // A_CHUNK
    chunks = [slice(c * A_CHUNK, (c + 1) * A_CHUNK) for c in range(nc)]
    fx = f_ref[0].astype(F32)
    f = lb + (1.0 - lb) * _sigmoid(fx)
    g = jnp.maximum(jnp.log(f), -128.0)
    kk = 1.0 - f
    ngrp = A_CHUNK // SUBLANES
    grouped = (nc, ngrp, SUBLANES, LANES)
    rowm = lax.broadcasted_iota(jnp.int32, grouped, 2)
    b4 = g.reshape(grouped)
    for s in (1, 2, 4):
        b4 = b4 + jnp.where(rowm >= s, pltpu.roll(b4, s, 2), 0.0)
    parts = [b4[:, 0]]
    for j in range(1, ngrp):
        carry = jnp.broadcast_to(parts[-1][:, SUBLANES - 1:SUBLANES, :], (nc, SUBLANES, LANES))
        parts.append(b4[:, j] + carry)
    b = jnp.stack(parts, axis=1).reshape(tt, LANES)
    q = q_ref[0].astype(F32)
    qs = (q * jnp.exp(b)).astype(BF16)
    b_ref[...] = b
    k_ref[...] = kk
    causal = _tril(A_CHUNK)
    small_decay = jnp.min(b) >= -A_MAX_DECAY

    @pl.when(small_decay)
    def _():
        kt = (kk * jnp.exp(-b)).astype(BF16)
        for rows in chunks:
            a = lax.dot_general(qs[rows], kt[rows], _NT, preferred_element_type=F32)
            a_ref[rows, :] = jnp.where(causal, a, 0.0)

    @pl.when(jnp.logical_not(small_decay))
    def _():
        coli = lax.broadcasted_iota(jnp.int32, (A_CHUNK, A_CHUNK), 1)

        def chunk_body(c, carry):
            r0 = pl.multiple_of(c * A_CHUNK, A_CHUNK)
            rows = pl.ds(r0, A_CHUNK)
            bc = b_ref[rows, :]
            qc = q_ref[0, rows, :].astype(F32)

            def col_body(j, acc):
                bj = b_ref[pl.ds(r0 + j, 1), :]
                kj = k_ref[pl.ds(r0 + j, 1), :]
                p = qc * jnp.exp(jnp.minimum(bc - bj, 0.0)) * kj
                return jnp.where(coli == j, jnp.sum(p, axis=-1, keepdims=True), acc)

            acc = lax.fori_loop(0, A_CHUNK, col_body, jnp.zeros((A_CHUNK, A_CHUNK), F32))
            a_ref[rows, :] = jnp.where(causal, acc, 0.0)
            return carry

        lax.fori_loop(0, nc, chunk_body, 0)

    v = i_ref[0]
    b_last = [b[r.stop - 1:r.stop, :] for r in chunks]
    bl_full = jnp.concatenate([jnp.broadcast_to(bl, (A_CHUNK, LANES)) for bl in b_last], axis=0)
    kd = (kk * jnp.exp(bl_full - b)).astype(BF16)
    o_intra = [jnp.dot(a_ref[r, :].astype(BF16), v[r], preferred_element_type=F32) for r in chunks]
    kv = [lax.dot_general(v[r], kd[r], _TN, preferred_element_type=F32) for r in chunks]
    st = st_ref[...]
    outs = []
    for c, r in enumerate(chunks):
        outs.append(o_intra[c] + lax.dot_general(qs[r], st.astype(BF16), _NT, preferred_element_type=F32))
        st = st * jnp.exp(b_last[c]) + kv[c]
    st_ref[...] = st
    o = jnp.concatenate(outs, axis=0)
    ms = jnp.mean(o * o, axis=-1, keepdims=True)
    y = o * lax.rsqrt(ms + EPS) * nw
    o_ref[0] = (y * _silu(z_ref[0].astype(F32))).astype(o_ref.dtype)


def _hgrn2(proj, lbl, nw, layer, tt=1024):
    nt = SEQ // tt
    blk = lambda off: pl.BlockSpec((1, tt, LANES), lambda b, hd, t: (off + hd, b * nt + t, 0))
    return pl.pallas_call(
        functools.partial(_hgrn_kernel, layer),
        grid=(BATCH, A_HEADS, nt),
        in_specs=[blk(BLK_AQ), blk(BLK_AF), blk(BLK_AI), blk(BLK_AZ),
                  pl.BlockSpec((1, DEPTH, LANES), lambda b, hd, t: (hd, 0, 0)),
                  pl.BlockSpec((None, 1, 1, LANES), lambda b, hd, t: (layer, hd, 0, 0))],
        out_specs=pl.BlockSpec((1, tt, LANES), lambda b, hd, t: (hd, b * nt + t, 0)),
        out_shape=jax.ShapeDtypeStruct((A_HEADS, TOKENS, LANES), BF16),
        scratch_shapes=[pltpu.VMEM((LANES, LANES), F32),
                        pltpu.VMEM((tt, LANES), F32),
                        pltpu.VMEM((tt, LANES), F32),
                        pltpu.VMEM((tt, A_CHUNK), F32)],
        compiler_params=_cparams(3),
        name="hgrn2",
    )(proj, proj, proj, proj, lbl, nw)


def _causal_conv(xbuf, x, cw, cb, first):
    tt = x.shape[0]

    @pl.when(first)
    def _():
        xbuf[0:CONV_HALO, :] = jnp.zeros((CONV_HALO, LANES), F32)

    xbuf[CONV_HALO:CONV_HALO + tt, :] = x
    y = cb
    for k in range(B_CONV):
        y = y + cw[k:k + 1, :] * xbuf[pl.ds(CONV_HALO - (B_CONV - 1) + k, tt), :]
    xbuf[0:CONV_HALO, :] = x[tt - CONV_HALO:tt, :]
    return y


def _rglru_kernel(x_ref, z_ref, cw_ref, cb_ref, wa_ref, ba_ref, wx_ref, bx_ref, lam_ref, o_ref,
                  xbuf, a_s, u_s, hcar):
    tt = x_ref.shape[1]
    first = pl.program_id(1) == 0

    @pl.when(first)
    def _():
        hcar[...] = jnp.zeros_like(hcar)

    grouped = (tt // SUBLANES, SUBLANES, LANES)
    rowm = lax.broadcasted_iota(jnp.int32, grouped, 1)
    for h in range(B_HEADS):
        xc = _causal_conv(xbuf.at[h], x_ref[h].astype(F32), cw_ref[h], cb_ref[h], first)
        xcb = xc.astype(BF16)
        r = _sigmoid(jnp.dot(xcb, wa_ref[h], preferred_element_type=F32) + ba_ref[h])
        ig = _sigmoid(jnp.dot(xcb, wx_ref[h], preferred_element_type=F32) + bx_ref[h])
        log_a = (-B_C) * r * _softplus(-lam_ref[h])
        a = jnp.exp(log_a)
        y = 1.0 - jnp.exp(2.0 * log_a)
        u = (y * lax.rsqrt(jnp.maximum(y, TINY))) * (ig * xc)
        a = a.reshape(grouped)
        u = u.reshape(grouped)
        for s in (1, 2, 4):
            a_sh = pltpu.roll(a, s, 1)
            u_sh = pltpu.roll(u, s, 1)
            m = rowm >= s
            u = jnp.where(m, a * u_sh + u, u)
            a = jnp.where(m, a * a_sh, a)
        a_s[h] = a.reshape(tt, LANES)
        u_s[h] = u.reshape(tt, LANES)

    def grp(g, hs):
        rows = pl.ds(pl.multiple_of(g * SUBLANES, SUBLANES), SUBLANES)
        new = []
        for h in range(B_HEADS):
            hg = a_s[h, rows, :] * hs[h] + u_s[h, rows, :]
            u_s[h, rows, :] = hg
            new.append(jnp.broadcast_to(hg[SUBLANES - 1:SUBLANES, :], (SUBLANES, LANES)))
        return tuple(new)

    hs = lax.fori_loop(0, tt // SUBLANES, grp, tuple(hcar[h] for h in range(B_HEADS)), unroll=4)
    for h in range(B_HEADS):
        hcar[h] = hs[h]
        o_ref[h] = (u_s[h] * _silu(z_ref[h].astype(F32))).astype(o_ref.dtype)


def _rglru(proj, cw, cb, wa, ba, wx, bx, lam, layer, tt=512):
    nt = SEQ // tt
    rows = lambda b, t: b * nt + t
    blk = lambda off: pl.BlockSpec((B_HEADS, tt, LANES), lambda b, t: (off // B_HEADS, rows(b, t), 0))
    par = lambda r, c: pl.BlockSpec((None, B_HEADS, r, c), lambda b, t: (layer, 0, 0, 0))
    return pl.pallas_call(
        _rglru_kernel,
        grid=(BATCH, nt),
        in_specs=[blk(BLK_BX), blk(BLK_BZ), par(B_CONV, LANES), par(1, LANES),
                  par(LANES, LANES), par(1, LANES), par(LANES, LANES), par(1, LANES), par(1, LANES)],
        out_specs=pl.BlockSpec((B_HEADS, tt, LANES), lambda b, t: (0, rows(b, t), 0)),
        out_shape=jax.ShapeDtypeStruct((B_HEADS, TOKENS, LANES), BF16),
        scratch_shapes=[pltpu.VMEM((B_HEADS, tt + CONV_HALO, LANES), F32),
                        pltpu.VMEM((B_HEADS, tt, LANES), F32),
                        pltpu.VMEM((B_HEADS, tt, LANES), F32),
                        pltpu.VMEM((B_HEADS, SUBLANES, LANES), F32)],
        compiler_params=_cparams(2),
        name="rglru",
    )(proj, proj, cw, cb, wa, ba, wx, bx, lam)


def _ssd_kernel(z_ref, x_ref, bm_ref, cm_ref, dt_ref, cwx_ref, cwb_ref, cwc_ref, cbx_ref, cbb_ref, cbc_ref,
                dtb_ref, alog_ref, dsk_ref, nw_ref, o_ref, xbuf, st_ref):
    tt = x_ref.shape[1]
    nxb = C_XBLOCKS
    first = pl.program_id(1) == 0

    @pl.when(first)
    def _():
        st_ref[...] = jnp.zeros_like(st_ref)

    dt_all = _softplus(dt_ref[0] + dtb_ref[...])
    da_all = dt_all * (-jnp.exp(alog_ref[...]))
    causal = _tril(C_CHUNK)
    causal_b = causal.astype(BF16)
    lane = lax.broadcasted_iota(jnp.int32, (C_CHUNK, LANES), 1)
    er = lax.broadcasted_iota(jnp.int32, (2 * LANES, C_GROUP_W), 0) & (LANES - 1)
    ec = lax.broadcasted_iota(jnp.int32, (2 * LANES, C_GROUP_W), 1)

    xs_all, bm_all, cm_all, spreads, dsk, nw, st = [], [], [], [], [], [], []
    for g in range(C_GROUPS):
        xs_all.append([_silu(_causal_conv(xbuf.at[g * nxb + p], x_ref[g * nxb + p].astype(F32),
                                          cwx_ref[g * nxb + p], cbx_ref[g * nxb + p], first)) for p in range(nxb)])
        bm_all.append(_silu(_causal_conv(xbuf.at[2 * nxb + g], bm_ref[g].astype(F32), cwb_ref[g], cbb_ref[g],
                                         first)).astype(BF16))
        cm_all.append(_silu(_causal_conv(xbuf.at[2 * nxb + C_GROUPS + g], cm_ref[g].astype(F32), cwc_ref[g],
                                         cbc_ref[g], first)).astype(BF16))
        ej = er - g * C_GROUP_HEADS
        spreads.append(((ec >= ej * C_HDIM) & (ec < (ej + 1) * C_HDIM)).astype(BF16))
        dsk.append(jnp.concatenate([dsk_ref[g * nxb + p] for p in range(nxb)], axis=1))
        nw.append(jnp.concatenate([nw_ref[g * nxb + p] for p in range(nxb)], axis=1))
        st.append(st_ref[g])

    for c in range(tt // C_CHUNK):
        rows = slice(c * C_CHUNK, (c + 1) * C_CHUNK)
        dt = dt_all[rows]
        cum = _dot_exact_lhs(causal_b, da_all[rows])
        cum_t = cum.T
        dt_t = dt.T
        ecum = jnp.exp(cum)
        cl = cum[C_CHUNK - 1:C_CHUNK, :]
        w = dt * jnp.exp(cl - cum)
        ecl = jnp.broadcast_to(jnp.exp(cl), (SUBLANES, LANES))
        for g in range(C_GROUPS):
            cmb = cm_all[g][rows]
            bmb = bm_all[g][rows]
            cb = lax.dot_general(cmb, bmb, _NT, preferred_element_type=F32)
            y_blocks = []
            for p in range(nxb):
                xb = xs_all[g][p][rows]
                acc = None
                for half in range(2):
                    hl = g * C_GROUP_HEADS + 2 * p + half
                    seg = cum[:, hl:hl + 1] - cum_t[hl:hl + 1, :]
                    lmat = jnp.exp(jnp.where(causal, seg, -1e30))
                    m = (cb * lmat * dt_t[hl:hl + 1, :]).astype(BF16)
                    in_head = (lane >= C_HDIM) if half else (lane < C_HDIM)
                    rhs = jnp.where(in_head, xb, 0.0).astype(BF16)
                    part = jnp.dot(m, rhs, preferred_element_type=F32)
                    acc = part if acc is None else acc + part
                y_blocks.append(acc)
            xs = jnp.concatenate([xs_all[g][p][rows] for p in range(nxb)], axis=1)
            y = jnp.concatenate(y_blocks, axis=1)
            y = y + (jnp.dot(cmb, st[g].astype(BF16), preferred_element_type=F32)
                     * _dot_hilo_rhs(ecum, spreads[g]))
            y = y + dsk[g] * xs
            xw = (xs * _dot_hilo_rhs(w, spreads[g])).astype(BF16)
            dec = _dot_hilo_rhs(ecl, spreads[g])[0:1, :]
            st[g] = st[g] * dec + lax.dot_general(bmb, xw, _TN, preferred_element_type=F32)
            zf = jnp.concatenate([z_ref[g * nxb + p, rows, :].astype(F32) for p in range(nxb)], axis=1)
            yg = y * _silu(zf)
            ms = jnp.mean(yg * yg, axis=-1, keepdims=True)
            out = yg * lax.rsqrt(ms + EPS) * nw[g]
            for p in range(nxb):
                o_ref[g * nxb + p, rows, :] = out[:, p * LANES:(p + 1) * LANES].astype(o_ref.dtype)

    for g in range(C_GROUPS):
        st_ref[g] = st[g]


def _ssd(proj, dtp, cw, cb, dtb, alog, dsk, nw, layer, tt=4 * C_CHUNK):
    nt = SEQ // tt
    nx = C_GROUPS * C_XBLOCKS
    rows = lambda b, t: b * nt + t
    big = lambda off: pl.BlockSpec((nx, tt, LANES), lambda b, t: (off // nx, rows(b, t), 0))
    two = lambda off: pl.BlockSpec((C_GROUPS, tt, LANES), lambda b, t: (off // C_GROUPS, rows(b, t), 0))
    par = lambda n, r, blk: pl.BlockSpec((None, n, r, LANES), lambda b, t: (layer, blk, 0, 0))
    vec = pl.BlockSpec((None, 1, LANES), lambda b, t: (layer, 0, 0))
    return pl.pallas_call(
        _ssd_kernel,
        grid=(BATCH, nt),
        in_specs=[big(BLK_CZ), big(BLK_CX), two(BLK_CB), two(BLK_CC),
                  pl.BlockSpec((1, tt, LANES), lambda b, t: (0, rows(b, t), 0)),
                  par(nx, C_CONV, 0), par(C_GROUPS, C_CONV, nx // C_GROUPS), par(C_GROUPS, C_CONV, nx // C_GROUPS + 1),
                  par(nx, 1, 0), par(C_GROUPS, 1, nx // C_GROUPS), par(C_GROUPS, 1, nx // C_GROUPS + 1),
                  vec, vec, par(nx, 1, 0), par(nx, 1, 0)],
        out_specs=pl.BlockSpec((nx, tt, LANES), lambda b, t: (0, rows(b, t), 0)),
        out_shape=jax.ShapeDtypeStruct((nx, TOKENS, LANES), BF16),
        scratch_shapes=[pltpu.VMEM((nx + 2 * C_GROUPS, tt + CONV_HALO, LANES), F32),
                        pltpu.VMEM((C_GROUPS, C_STATE, C_GROUP_W), F32)],
        compiler_params=_cparams(2),
        name="ssd",
    )(proj, proj, proj, proj, dtp, cw, cw, cw, cb, cb, cb, dtb, alog, dsk, nw)


def _merge_kernel(final, oa_ref, ob_ref, oc_ref, g_ref, x_ref, w_ref, nw_ref, *out_refs):
    nb = WIDTH // LANES
    acc = x_ref[...]
    for br, o_ref in enumerate((oa_ref, ob_ref, oc_ref)):
        o = jnp.concatenate([o_ref[hh] for hh in range(nb)], axis=1)
        bo = jnp.dot(o, w_ref[br], preferred_element_type=F32)
        gl = jnp.concatenate([g_ref[br * GATE_BLOCKS + c] for c in range(GATE_BLOCKS)], axis=1).astype(F32)
        acc = acc + _sigmoid(gl) * bo
    ms = jnp.mean(acc * acc, axis=-1, keepdims=True)
    hn = acc * lax.rsqrt(ms + EPS) * nw_ref[...]
    if final:
        out_refs[0][...] = hn.astype(out_refs[0].dtype)
    else:
        out_refs[0][...] = acc
        out_refs[1][...] = hn.astype(out_refs[1].dtype)


def _merge(oa, ob, oc, gates, x, w_out, nw, layer, final, tm=256):
    nb = WIDTH // LANES
    obs = pl.BlockSpec((nb, tm, LANES), lambda i: (0, i, 0))
    gbs = pl.BlockSpec((N_BRANCH * GATE_BLOCKS, tm, LANES), lambda i: (0, i, 0))
    row = pl.BlockSpec((tm, D_MODEL), lambda i: (i, 0))
    if final:
        out_specs = row
        out_shape = jax.ShapeDtypeStruct((TOKENS, D_MODEL), F32)
    else:
        out_specs = [row, row]
        out_shape = [jax.ShapeDtypeStruct((TOKENS, D_MODEL), F32),
                     jax.ShapeDtypeStruct((TOKENS, D_MODEL), BF16)]
    return pl.pallas_call(
        functools.partial(_merge_kernel, final),
        grid=(TOKENS // tm,),
        in_specs=[obs, obs, obs, gbs, row,
                  pl.BlockSpec((None, N_BRANCH, WIDTH, D_MODEL), lambda i: (layer, 0, 0, 0)),
                  pl.BlockSpec((1, D_MODEL), lambda i: (0, 0))],
        out_specs=out_specs,
        out_shape=out_shape,
        compiler_params=_cparams(1),
        name="merge",
    )(oa, ob, oc, gates, x, w_out, nw)


def _blocks(p, rows):
    d, r, n = p.shape
    return p.reshape(d, r, n // LANES, LANES).transpose(0, 2, 1, 3)


def _pad_lanes(p):
    return jnp.pad(p, ((0, 0),) * (p.ndim - 1) + ((0, LANES - p.shape[-1]),))


def kernel(x, norm_w, w_in, hgrn_lb_logits, hgrn_norm_w, rglru_conv_w, rglru_conv_b, rglru_wa, rglru_ba,
           rglru_wx, rglru_bx, rglru_lambda, ssd_conv_w, ssd_conv_b, ssd_dt_bias, ssd_a_log, ssd_d,
           ssd_norm_w, w_out, final_norm_w):
    w_t = jnp.swapaxes(w_in, 1, 2)
    wg_t = w_t[:, GATE_COL0:, :].astype(BF16)
    w_o = w_out.reshape(DEPTH, N_BRANCH, WIDTH, D_MODEL).astype(BF16)
    lbl = hgrn_lb_logits.reshape(DEPTH, A_HEADS, LANES).transpose(1, 0, 2)
    a_nw = _blocks(hgrn_norm_w[:, None, :], 1)
    b_cw = _blocks(rglru_conv_w, B_CONV)
    b_cb = _blocks(rglru_conv_b[:, None, :], 1)
    b_wa = rglru_wa.astype(BF16)
    b_wx = rglru_wx.astype(BF16)
    b_ba = rglru_ba[:, :, None, :]
    b_bx = rglru_bx[:, :, None, :]
    b_lam = _blocks(rglru_lambda[:, None, :], 1)
    c_cw = _blocks(ssd_conv_w, C_CONV)
    c_cb = _blocks(ssd_conv_b[:, None, :], 1)
    c_dtb = _pad_lanes(ssd_dt_bias)[:, None, :]
    c_alog = _pad_lanes(ssd_a_log)[:, None, :]
    c_dsk = _blocks(jnp.repeat(ssd_d, C_HDIM, axis=1)[:, None, :], 1)
    c_nw = _blocks(ssd_norm_w[:, None, :], 1)

    xf = x.reshape(TOKENS, D_MODEL)
    h = _rmsnorm(xf, norm_w[0:1], BF16)
    for l in range(DEPTH):
        proj, dtp = _project(h, w_t, l)
        gates = _project_gates(h, wg_t, l)
        oa = _hgrn2(proj, lbl, a_nw, l)
        ob = _rglru(proj, b_cw, b_cb, b_wa, b_ba, b_wx, b_bx, b_lam, l)
        oc = _ssd(proj, dtp, c_cw, c_cb, c_dtb, c_alog, c_dsk, c_nw, l)
        if l + 1 < DEPTH:
            xf, h = _merge(oa, ob, oc, gates, xf, w_o, norm_w[l + 1:l + 2], l, final=False)
        else:
            out = _merge(oa, ob, oc, gates, xf, w_o, final_norm_w[None, :], l, final=True)
    return out.reshape(BATCH, SEQ, D_MODEL)
```

```python
import functools

import jax
import jax.numpy as jnp
from jax import lax
from jax.experimental import pallas as pl
from jax.experimental.pallas import tpu as pltpu

F32 = jnp.float32
BF16 = jnp.bfloat16

D_MODEL = 2048
BATCH = 2
SEQ = 4096
DEPTH = 4
TOKENS = BATCH * SEQ
EPS = 1e-6
WIDTH = 1024
N_BRANCH = 3
LANES = 128
SUBLANES = 8
A_HEADS = 8
A_CHUNK = 64
A_MAX_DECAY = 80.0
B_HEADS = 8
B_CONV = 4
B_C = 8.0
C_HDIM = 64
C_HEADS = 16
C_GROUPS = 2
C_STATE = 128
C_CONV = 4
C_CHUNK = 128
C_GROUP_W = WIDTH // C_GROUPS
C_GROUP_HEADS = C_HEADS // C_GROUPS
C_XBLOCKS = C_GROUP_W // LANES
MAIN_COLS = 8 * WIDTH + 2 * C_GROUPS * C_STATE
GATE_COL0 = MAIN_COLS + C_HEADS
GATE_SHIFT = GATE_COL0 - MAIN_COLS
PROJ_TN = 512
PROJ_TILES = MAIN_COLS // PROJ_TN
TILE_BLOCKS = PROJ_TN // LANES
BLK_AQ, BLK_AF, BLK_AI, BLK_AZ = 0, 8, 16, 24
BLK_BX, BLK_BZ = 32, 40
BLK_CZ, BLK_CX, BLK_CB, BLK_CC = 48, 56, 64, 66
MAIN_BLOCKS = MAIN_COLS // LANES
GATE_BLOCKS = D_MODEL // LANES
CONV_HALO = SUBLANES
VMEM_LIMIT = 56 * 1024 * 1024

NEG_LOG2E = -1.4426950408889634
TINY = 1e-30
_NT = (((1,), (1,)), ((), ()))
_TN = (((0,), (0,)), ((), ()))


def _cparams(n_axes):
    return pltpu.CompilerParams(dimension_semantics=("arbitrary",) * n_axes,
                                vmem_limit_bytes=VMEM_LIMIT)


def _sigmoid(x):
    return 1.0 / (1.0 + jnp.exp2(x * NEG_LOG2E))


def _silu(x):
    return x * _sigmoid(x)


def _softplus(x):
    return jnp.maximum(x, 0.0) + jnp.log1p(jnp.exp(-jnp.abs(x)))


def _dot_exact_lhs(m01, x):
    x1 = x.astype(BF16)
    r1 = x - x1.astype(F32)
    x2 = r1.astype(BF16)
    x3 = (r1 - x2.astype(F32)).astype(BF16)
    d = lambda a: jnp.dot(m01, a, preferred_element_type=F32)
    return d(x1) + d(x2) + d(x3)


def _dot_hilo_rhs(x, m01_twice):
    x1 = x.astype(BF16)
    x2 = (x - x1.astype(F32)).astype(BF16)
    return jnp.dot(jnp.concatenate([x1, x2], axis=1), m01_twice, preferred_element_type=F32)


def _tril(n):
    r = lax.broadcasted_iota(jnp.int32, (n, n), 0)
    c = lax.broadcasted_iota(jnp.int32, (n, n), 1)
    return r >= c


def _rms_kernel(x_ref, w_ref, o_ref):
    x = x_ref[...]
    ms = jnp.mean(x * x, axis=-1, keepdims=True)
    o_ref[...] = (x * lax.rsqrt(ms + EPS) * w_ref[...]).astype(o_ref.dtype)


def _rmsnorm(x, w, out_dtype, tm=512):
    t, d = x.shape
    return pl.pallas_call(
        _rms_kernel,
        grid=(t // tm,),
        in_specs=[pl.BlockSpec((tm, d), lambda i: (i, 0)),
                  pl.BlockSpec((1, d), lambda i: (0, 0))],
        out_specs=pl.BlockSpec((tm, d), lambda i: (i, 0)),
        out_shape=jax.ShapeDtypeStruct((t, d), out_dtype),
        compiler_params=_cparams(1),
        name="rmsnorm",
    )(x, w)


def _proj_kernel(h_ref, w_ref, wdt_ref, o_ref, dt_ref):
    h = h_ref[...]
    acc = lax.dot_general(h, w_ref[...].astype(BF16), _NT, preferred_element_type=F32)
    for c in range(TILE_BLOCKS):
        o_ref[c] = acc[:, c * LANES:(c + 1) * LANES].astype(o_ref.dtype)

    @pl.when(pl.program_id(1) == PROJ_TILES - 1)
    def _():
        dt_ref[0] = lax.dot_general(h, wdt_ref[...].astype(BF16), _NT, preferred_element_type=F32)


def _project(h, w_t, layer, tm=2048):
    t, d = h.shape
    return pl.pallas_call(
        _proj_kernel,
        grid=(t // tm, PROJ_TILES),
        in_specs=[pl.BlockSpec((tm, d), lambda i, j: (i, 0)),
                  pl.BlockSpec((None, PROJ_TN, d), lambda i, j: (layer, j, 0)),
                  pl.BlockSpec((None, LANES, d), lambda i, j: (layer, MAIN_COLS // LANES, 0))],
        out_specs=[pl.BlockSpec((TILE_BLOCKS, tm, LANES), lambda i, j: (j, i, 0)),
                   pl.BlockSpec((1, tm, LANES), lambda i, j: (0, i, 0))],
        out_shape=[jax.ShapeDtypeStruct((MAIN_BLOCKS, t, LANES), BF16),
                   jax.ShapeDtypeStruct((1, t, LANES), F32)],
        compiler_params=_cparams(2),
        name="proj_in",
    )(h, w_t, w_t)


def _gate_kernel(h_ref, wa_ref, wb_ref, o_ref):
    w = jnp.concatenate([wa_ref[GATE_SHIFT:, :], wb_ref[...]], axis=0).astype(BF16)
    acc = lax.dot_general(h_ref[...], w, _NT, preferred_element_type=F32)
    for c in range(TILE_BLOCKS):
        o_ref[c] = acc[:, c * LANES:(c + 1) * LANES].astype(o_ref.dtype)


def _project_gates(h, w_t, layer, tm=2048):
    t, d = h.shape
    n_tiles = N_BRANCH * D_MODEL // PROJ_TN
    per_tile = PROJ_TN // GATE_SHIFT
    return pl.pallas_call(
        _gate_kernel,
        grid=(t // tm, n_tiles),
        in_specs=[pl.BlockSpec((tm, d), lambda i, j: (i, 0)),
                  pl.BlockSpec((None, PROJ_TN, d), lambda i, j: (layer, PROJ_TILES + j, 0)),
                  pl.BlockSpec((None, GATE_SHIFT, d), lambda i, j: (layer, (PROJ_TILES + j + 1) * per_tile, 0))],
        out_specs=pl.BlockSpec((TILE_BLOCKS, tm, LANES), lambda i, j: (j, i, 0)),
        out_shape=jax.ShapeDtypeStruct((n_tiles * TILE_BLOCKS, t, LANES), BF16),
        compiler_params=_cparams(2),
        name="proj_gate",
    )(h, w_t, w_t)


def _hgrn_kernel(layer, q_ref, f_ref, i_ref, z_ref, lbl_ref, nw_ref, o_ref, st_ref, b_ref, k_ref, a_ref):
    tt = q_ref.shape[1]

    @pl.when(pl.program_id(2) == 0)
    def _():
        st_ref[...] = jnp.zeros_like(st_ref)

    lbl = lbl_ref[0]
    e = jnp.exp(lbl - jnp.max(lbl, axis=0, keepdims=True))
    lb = jnp.zeros((1, LANES), F32)
    for j in range(1, layer + 1):
        lb = lb + e[j:j + 1, :]
    lb = lb / jnp.sum(e, axis=0, keepdims=True)
    nw = nw_ref[0]

    nc = tt // A_CHUNK
    chunks = [slice(c * A_CHUNK, (c + 1) * A_CHUNK) for c in range(nc)]
    fx = f_ref[0].astype(F32)
    f = lb + (1.0 - lb) * _sigmoid(fx)
    g = jnp.maximum(jnp.log(f), -128.0)
    kk = 1.0 - f
    ngrp = A_CHUNK // SUBLANES
    grouped = (nc, ngrp, SUBLANES, LANES)
    rowm = lax.broadcasted_iota(jnp.int32, grouped, 2)
    b4 = g.reshape(grouped)
    for s in (1, 2, 4):
        b4 = b4 + jnp.where(rowm >= s, pltpu.roll(b4, s, 2), 0.0)
    parts = [b4[:, 0]]
    for j in range(1, ngrp):
        carry = jnp.broadcast_to(parts[-1][:, SUBLANES - 1:SUBLANES, :], (nc, SUBLANES, LANES))
        parts.append(b4[:, j] + carry)
    b = jnp.stack(parts, axis=1).reshape(tt, LANES)
    q = q_ref[0].astype(F32)
    qs = (q * jnp.exp(b)).astype(BF16)
    b_ref[...] = b
    k_ref[...] = kk
    causal = _tril(A_CHUNK)
    small_decay = jnp.min(b) >= -A_MAX_DECAY

    @pl.when(small_decay)
    def _():
        kt = (kk * jnp.exp(-b)).astype(BF16)
        for rows in chunks:
            a = lax.dot_general(qs[rows], kt[rows], _NT, preferred_element_type=F32)
            a_ref[rows, :] = jnp.where(causal, a, 0.0)

    @pl.when(jnp.logical_not(small_decay))
    def _():
        coli = lax.broadcasted_iota(jnp.int32, (A_CHUNK, A_CHUNK), 1)

        def chunk_body(c, carry):
            r0 = pl.multiple_of(c * A_CHUNK, A_CHUNK)
            rows = pl.ds(r0, A_CHUNK)
            bc = b_ref[rows, :]
            qc = q_ref[0, rows, :].astype(F32)

            def col_body(j, acc):
                bj = b_ref[pl.ds(r0 + j, 1), :]
                kj = k_ref[pl.ds(r0 + j, 1), :]
                p = qc * jnp.exp(jnp.minimum(bc - bj, 0.0)) * kj
                return jnp.where(coli == j, jnp.sum(p, axis=-1, keepdims=True), acc)

            acc = lax.fori_loop(0, A_CHUNK, col_body, jnp.zeros((A_CHUNK, A_CHUNK), F32))
            a_ref[rows, :] = jnp.where(causal, acc, 0.0)
            return carry

        lax.fori_loop(0, nc, chunk_body, 0)

    v = i_ref[0]
    b_last = [b[r.stop - 1:r.stop, :] for r in chunks]
    bl_full = jnp.concatenate([jnp.broadcast_to(bl, (A_CHUNK, LANES)) for bl in b_last], axis=0)
    kd = (kk * jnp.exp(bl_full - b)).astype(BF16)
    o_intra = [jnp.dot(a_ref[r, :].astype(BF16), v[r], preferred_element_type=F32) for r in chunks]
    kv = [lax.dot_general(v[r], kd[r], _TN, preferred_element_type=F32) for r in chunks]
    st = st_ref[...]
    outs = []
    for c, r in enumerate(chunks):
        outs.append(o_intra[c] + lax.dot_general(qs[r], st.astype(BF16), _NT, preferred_element_type=F32))
        st = st * jnp.exp(b_last[c]) + kv[c]
    st_ref[...] = st
    o = jnp.concatenate(outs, axis=0)
    ms = jnp.mean(o * o, axis=-1, keepdims=True)
    y = o * lax.rsqrt(ms + EPS) * nw
    o_ref[0] = (y * _silu(z_ref[0].astype(F32))).astype(o_ref.dtype)


def _hgrn2(proj, lbl, nw, layer, tt=1024):
    nt = SEQ // tt
    blk = lambda off: pl.BlockSpec((1, tt, LANES), lambda b, hd, t: (off + hd, b * nt + t, 0))
    return pl.pallas_call(
        functools.partial(_hgrn_kernel, layer),
        grid=(BATCH, A_HEADS, nt),
        in_specs=[blk(BLK_AQ), blk(BLK_AF), blk(BLK_AI), blk(BLK_AZ),
                  pl.BlockSpec((1, DEPTH, LANES), lambda b, hd, t: (hd, 0, 0)),
                  pl.BlockSpec((None, 1, 1, LANES), lambda b, hd, t: (layer, hd, 0, 0))],
        out_specs=pl.BlockSpec((1, tt, LANES), lambda b, hd, t: (hd, b * nt + t, 0)),
        out_shape=jax.ShapeDtypeStruct((A_HEADS, TOKENS, LANES), BF16),
        scratch_shapes=[pltpu.VMEM((LANES, LANES), F32),
                        pltpu.VMEM((tt, LANES), F32),
                        pltpu.VMEM((tt, LANES), F32),
                        pltpu.VMEM((tt, A_CHUNK), F32)],
        compiler_params=_cparams(3),
        name="hgrn2",
    )(proj, proj, proj, proj, lbl, nw)


def _causal_conv(xbuf, x, cw, cb, first):
    tt = x.shape[0]

    @pl.when(first)
    def _():
        xbuf[0:CONV_HALO, :] = jnp.zeros((CONV_HALO, LANES), F32)

    xbuf[CONV_HALO:CONV_HALO + tt, :] = x
    y = cb
    for k in range(B_CONV):
        y = y + cw[k:k + 1, :] * xbuf[pl.ds(CONV_HALO - (B_CONV - 1) + k, tt), :]
    xbuf[0:CONV_HALO, :] = x[tt - CONV_HALO:tt, :]
    return y


def _rglru_kernel(x_ref, z_ref, cw_ref, cb_ref, wa_ref, ba_ref, wx_ref, bx_ref, lam_ref, o_ref,
                  xbuf, a_s, u_s, hcar):
    tt = x_ref.shape[1]
    first = pl.program_id(1) == 0

    @pl.when(first)
    def _():
        hcar[...] = jnp.zeros_like(hcar)

    grouped = (tt // SUBLANES, SUBLANES, LANES)
    rowm = lax.broadcasted_iota(jnp.int32, grouped, 1)
    for h in range(B_HEADS):
        xc = _causal_conv(xbuf.at[h], x_ref[h].astype(F32), cw_ref[h], cb_ref[h], first)
        xcb = xc.astype(BF16)
        r = _sigmoid(jnp.dot(xcb, wa_ref[h], preferred_element_type=F32) + ba_ref[h])
        ig = _sigmoid(jnp.dot(xcb, wx_ref[h], preferred_element_type=F32) + bx_ref[h])
        log_a = (-B_C) * r * _softplus(-lam_ref[h])
        a = jnp.exp(log_a)
        y = 1.0 - jnp.exp(2.0 * log_a)
        u = (y * lax.rsqrt(jnp.maximum(y, TINY))) * (ig * xc)
        a = a.reshape(grouped)
        u = u.reshape(grouped)
        for s in (1, 2, 4):
            a_sh = pltpu.roll(a, s, 1)
            u_sh = pltpu.roll(u, s, 1)
            m = rowm >= s
            u = jnp.where(m, a * u_sh + u, u)
            a = jnp.where(m, a * a_sh, a)
        a_s[h] = a.reshape(tt, LANES)
        u_s[h] = u.reshape(tt, LANES)

    def grp(g, hs):
        rows = pl.ds(pl.multiple_of(g * SUBLANES, SUBLANES), SUBLANES)
        new = []
        for h in range(B_HEADS):
            hg = a_s[h, rows, :] * hs[h] + u_s[h, rows, :]
            u_s[h, rows, :] = hg
            new.append(jnp.broadcast_to(hg[SUBLANES - 1:SUBLANES, :], (SUBLANES, LANES)))
        return tuple(new)

    hs = lax.fori_loop(0, tt // SUBLANES, grp, tuple(hcar[h] for h in range(B_HEADS)), unroll=4)
    for h in range(B_HEADS):
        hcar[h] = hs[h]
        o_ref[h] = (u_s[h] * _silu(z_ref[h].astype(F32))).astype(o_ref.dtype)


def _rglru(proj, cw, cb, wa, ba, wx, bx, lam, layer, tt=512):
    nt = SEQ // tt
    rows = lambda b, t: b * nt + t
    blk = lambda off: pl.BlockSpec((B_HEADS, tt, LANES), lambda b, t: (off // B_HEADS, rows(b, t), 0))
    par = lambda r, c: pl.BlockSpec((None, B_HEADS, r, c), lambda b, t: (layer, 0, 0, 0))
    return pl.pallas_call(
        _rglru_kernel,
        grid=(BATCH, nt),
        in_specs=[blk(BLK_BX), blk(BLK_BZ), par(B_CONV, LANES), par(1, LANES),
                  par(LANES, LANES), par(1, LANES), par(LANES, LANES), par(1, LANES), par(1, LANES)],
        out_specs=pl.BlockSpec((B_HEADS, tt, LANES), lambda b, t: (0, rows(b, t), 0)),
        out_shape=jax.ShapeDtypeStruct((B_HEADS, TOKENS, LANES), BF16),
        scratch_shapes=[pltpu.VMEM((B_HEADS, tt + CONV_HALO, LANES), F32),
                        pltpu.VMEM((B_HEADS, tt, LANES), F32),
                        pltpu.VMEM((B_HEADS, tt, LANES), F32),
                        pltpu.VMEM((B_HEADS, SUBLANES, LANES), F32)],
        compiler_params=_cparams(2),
        name="rglru",
    )(proj, proj, cw, cb, wa, ba, wx, bx, lam)


def _ssd_kernel(z_ref, x_ref, bm_ref, cm_ref, dt_ref, cwx_ref, cwb_ref, cwc_ref, cbx_ref, cbb_ref, cbc_ref,
                dtb_ref, alog_ref, dsk_ref, nw_ref, o_ref, xbuf, st_ref):
    tt = x_ref.shape[1]
    nxb = C_XBLOCKS
    first = pl.program_id(1) == 0

    @pl.when(first)
    def _():
        st_ref[...] = jnp.zeros_like(st_ref)

    dt_all = _softplus(dt_ref[0] + dtb_ref[...])
    da_all = dt_all * (-jnp.exp(alog_ref[...]))
    causal = _tril(C_CHUNK)
    causal_b = causal.astype(BF16)
    lane = lax.broadcasted_iota(jnp.int32, (C_CHUNK, LANES), 1)
    er = lax.broadcasted_iota(jnp.int32, (2 * LANES, C_GROUP_W), 0) & (LANES - 1)
    ec = lax.broadcasted_iota(jnp.int32, (2 * LANES, C_GROUP_W), 1)

    xs_all, bm_all, cm_all, spreads, dsk, nw, st = [], [], [], [], [], [], []
    for g in range(C_GROUPS):
        xs_all.append([_silu(_causal_conv(xbuf.at[g * nxb + p], x_ref[g * nxb + p].astype(F32),
                                          cwx_ref[g * nxb + p], cbx_ref[g * nxb + p], first)) for p in range(nxb)])
        bm_all.append(_silu(_causal_conv(xbuf.at[2 * nxb + g], bm_ref[g].astype(F32), cwb_ref[g], cbb_ref[g],
                                         first)).astype(BF16))
        cm_all.append(_silu(_causal_conv(xbuf.at[2 * nxb + C_GROUPS + g], cm_ref[g].astype(F32), cwc_ref[g],
                                         cbc_ref[g], first)).astype(BF16))
        ej = er - g * C_GROUP_HEADS
        spreads.append(((ec >= ej * C_HDIM) & (ec < (ej + 1) * C_HDIM)).astype(BF16))
        dsk.append(jnp.concatenate([dsk_ref[g * nxb + p] for p in range(nxb)], axis=1))
        nw.append(jnp.concatenate([nw_ref[g * nxb + p] for p in range(nxb)], axis=1))
        st.append(st_ref[g])

    for c in range(tt // C_CHUNK):
        rows = slice(c * C_CHUNK, (c + 1) * C_CHUNK)
        dt = dt_all[rows]
        cum = _dot_exact_lhs(causal_b, da_all[rows])
        cum_t = cum.T
        dt_t = dt.T
        ecum = jnp.exp(cum)
        cl = cum[C_CHUNK - 1:C_CHUNK, :]
        w = dt * jnp.exp(cl - cum)
        ecl = jnp.broadcast_to(jnp.exp(cl), (SUBLANES, LANES))
        for g in range(C_GROUPS):
            cmb = cm_all[g][rows]
            bmb = bm_all[g][rows]
            cb = lax.dot_general(cmb, bmb, _NT, preferred_element_type=F32)
            y_blocks = []
            for p in range(nxb):
                xb = xs_all[g][p][rows]
                acc = None
                for half in range(2):
                    hl = g * C_GROUP_HEADS + 2 * p + half
                    seg = cum[:, hl:hl + 1] - cum_t[hl:hl + 1, :]
                    lmat = jnp.exp(jnp.where(causal, seg, -1e30))
                    m = (cb * lmat * dt_t[hl:hl + 1, :]).astype(BF16)
                    in_head = (lane >= C_HDIM) if half else (lane < C_HDIM)
                    rhs = jnp.where(in_head, xb, 0.0).astype(BF16)
                    part = jnp.dot(m, rhs, preferred_element_type=F32)
                    acc = part if acc is None else acc + part
                y_blocks.append(acc)
            xs = jnp.concatenate([xs_all[g][p][rows] for p in range(nxb)], axis=1)
            y = jnp.concatenate(y_blocks, axis=1)
            y = y + (jnp.dot(cmb, st[g].astype(BF16), preferred_element_type=F32)
                     * _dot_hilo_rhs(ecum, spreads[g]))
            y = y + dsk[g] * xs
            xw = (xs * _dot_hilo_rhs(w, spreads[g])).astype(BF16)
            dec = _dot_hilo_rhs(ecl, spreads[g])[0:1, :]
            st[g] = st[g] * dec + lax.dot_general(bmb, xw, _TN, preferred_element_type=F32)
            zf = jnp.concatenate([z_ref[g * nxb + p, rows, :].astype(F32) for p in range(nxb)], axis=1)
            yg = y * _silu(zf)
            ms = jnp.mean(yg * yg, axis=-1, keepdims=True)
            out = yg * lax.rsqrt(ms + EPS) * nw[g]
            for p in range(nxb):
                o_ref[g * nxb + p, rows, :] = out[:, p * LANES:(p + 1) * LANES].astype(o_ref.dtype)

    for g in range(C_GROUPS):
        st_ref[g] = st[g]


def _ssd(proj, dtp, cw, cb, dtb, alog, dsk, nw, layer, tt=4 * C_CHUNK):
    nt = SEQ // tt
    nx = C_GROUPS * C_XBLOCKS
    rows = lambda b, t: b * nt + t
    big = lambda off: pl.BlockSpec((nx, tt, LANES), lambda b, t: (off // nx, rows(b, t), 0))
    two = lambda off: pl.BlockSpec((C_GROUPS, tt, LANES), lambda b, t: (off // C_GROUPS, rows(b, t), 0))
    par = lambda n, r, blk: pl.BlockSpec((None, n, r, LANES), lambda b, t: (layer, blk, 0, 0))
    vec = pl.BlockSpec((None, 1, LANES), lambda b, t: (layer, 0, 0))
    return pl.pallas_call(
        _ssd_kernel,
        grid=(BATCH, nt),
        in_specs=[big(BLK_CZ), big(BLK_CX), two(BLK_CB), two(BLK_CC),
                  pl.BlockSpec((1, tt, LANES), lambda b, t: (0, rows(b, t), 0)),
                  par(nx, C_CONV, 0), par(C_GROUPS, C_CONV, nx // C_GROUPS), par(C_GROUPS, C_CONV, nx // C_GROUPS + 1),
                  par(nx, 1, 0), par(C_GROUPS, 1, nx // C_GROUPS), par(C_GROUPS, 1, nx // C_GROUPS + 1),
                  vec, vec, par(nx, 1, 0), par(nx, 1, 0)],
        out_specs=pl.BlockSpec((nx, tt, LANES), lambda b, t: (0, rows(b, t), 0)),
        out_shape=jax.ShapeDtypeStruct((nx, TOKENS, LANES), BF16),
        scratch_shapes=[pltpu.VMEM((nx + 2 * C_GROUPS, tt + CONV_HALO, LANES), F32),
                        pltpu.VMEM((C_GROUPS, C_STATE, C_GROUP_W), F32)],
        compiler_params=_cparams(2),
        name="ssd",
    )(proj, proj, proj, proj, dtp, cw, cw, cw, cb, cb, cb, dtb, alog, dsk, nw)


def _merge_kernel(final, oa_ref, ob_ref, oc_ref, g_ref, x_ref, w_ref, nw_ref, *out_refs):
    nb = WIDTH // LANES
    acc = x_ref[...]
    for br, o_ref in enumerate((oa_ref, ob_ref, oc_ref)):
        o = jnp.concatenate([o_ref[hh] for hh in range(nb)], axis=1)
        bo = jnp.dot(o, w_ref[br], preferred_element_type=F32)
        gl = jnp.concatenate([g_ref[br * GATE_BLOCKS + c] for c in range(GATE_BLOCKS)], axis=1).astype(F32)
        acc = acc + _sigmoid(gl) * bo
    ms = jnp.mean(acc * acc, axis=-1, keepdims=True)
    hn = acc * lax.rsqrt(ms + EPS) * nw_ref[...]
    if final:
        out_refs[0][...] = hn.astype(out_refs[0].dtype)
    else:
        out_refs[0][...] = acc
        out_refs[1][...] = hn.astype(out_refs[1].dtype)


def _merge(oa, ob, oc, gates, x, w_out, nw, layer, final, tm=256):
    nb = WIDTH // LANES
    obs = pl.BlockSpec((nb, tm, LANES), lambda i: (0, i, 0))
    gbs = pl.BlockSpec((N_BRANCH * GATE_BLOCKS, tm, LANES), lambda i: (0, i, 0))
    row = pl.BlockSpec((tm, D_MODEL), lambda i: (i, 0))
    if final:
        out_specs = row
        out_shape = jax.ShapeDtypeStruct((TOKENS, D_MODEL), F32)
    else:
        out_specs = [row, row]
        out_shape = [jax.ShapeDtypeStruct((TOKENS, D_MODEL), F32),
                     jax.ShapeDtypeStruct((TOKENS, D_MODEL), BF16)]
    return pl.pallas_call(
        functools.partial(_merge_kernel, final),
        grid=(TOKENS // tm,),
        in_specs=[obs, obs, obs, gbs, row,
                  pl.BlockSpec((None, N_BRANCH, WIDTH, D_MODEL), lambda i: (layer, 0, 0, 0)),
                  pl.BlockSpec((1, D_MODEL), lambda i: (0, 0))],
        out_specs=out_specs,
        out_shape=out_shape,
        compiler_params=_cparams(1),
        name="merge",
    )(oa, ob, oc, gates, x, w_out, nw)


def _blocks(p, rows):
    d, r, n = p.shape
    return p.reshape(d, r, n // LANES, LANES).transpose(0, 2, 1, 3)


def _pad_lanes(p):
    return jnp.pad(p, ((0, 0),) * (p.ndim - 1) + ((0, LANES - p.shape[-1]),))


def kernel(x, norm_w, w_in, hgrn_lb_logits, hgrn_norm_w, rglru_conv_w, rglru_conv_b, rglru_wa, rglru_ba,
           rglru_wx, rglru_bx, rglru_lambda, ssd_conv_w, ssd_conv_b, ssd_dt_bias, ssd_a_log, ssd_d,
           ssd_norm_w, w_out, final_norm_w):
    w_t = jnp.swapaxes(w_in, 1, 2)
    w_o = w_out.reshape(DEPTH, N_BRANCH, WIDTH, D_MODEL).astype(BF16)
    lbl = hgrn_lb_logits.reshape(DEPTH, A_HEADS, LANES).transpose(1, 0, 2)
    a_nw = _blocks(hgrn_norm_w[:, None, :], 1)
    b_cw = _blocks(rglru_conv_w, B_CONV)
    b_cb = _blocks(rglru_conv_b[:, None, :], 1)
    b_wa = rglru_wa.astype(BF16)
    b_wx = rglru_wx.astype(BF16)
    b_ba = rglru_ba[:, :, None, :]
    b_bx = rglru_bx[:, :, None, :]
    b_lam = _blocks(rglru_lambda[:, None, :], 1)
    c_cw = _blocks(ssd_conv_w, C_CONV)
    c_cb = _blocks(ssd_conv_b[:, None, :], 1)
    c_dtb = _pad_lanes(ssd_dt_bias)[:, None, :]
    c_alog = _pad_lanes(ssd_a_log)[:, None, :]
    c_dsk = _blocks(jnp.repeat(ssd_d, C_HDIM, axis=1)[:, None, :], 1)
    c_nw = _blocks(ssd_norm_w[:, None, :], 1)

    xf = x.reshape(TOKENS, D_MODEL)
    h = _rmsnorm(xf, norm_w[0:1], BF16)
    for l in range(DEPTH):
        proj, dtp = _project(h, w_t, l)
        gates = _project_gates(h, w_t, l)
        oa = _hgrn2(proj, lbl, a_nw, l)
        ob = _rglru(proj, b_cw, b_cb, b_wa, b_ba, b_wx, b_bx, b_lam, l)
        oc = _ssd(proj, dtp, c_cw, c_cb, c_dtb, c_alog, c_dsk, c_nw, l)
        if l + 1 < DEPTH:
            xf, h = _merge(oa, ob, oc, gates, xf, w_o, norm_w[l + 1:l + 2], l, final=False)
        else:
            out = _merge(oa, ob, oc, gates, xf, w_o, final_norm_w[None, :], l, final=True)
    return out.reshape(BATCH, SEQ, D_MODEL)
```

```python
import functools

import jax
import jax.numpy as jnp
from jax import lax
from jax.experimental import pallas as pl
from jax.experimental.pallas import tpu as pltpu

F32 = jnp.float32
BF16 = jnp.bfloat16

D_MODEL = 2048
BATCH = 2
SEQ = 4096
DEPTH = 4
TOKENS = BATCH * SEQ
EPS = 1e-6
WIDTH = 1024
N_BRANCH = 3
LANES = 128
SUBLANES = 8
A_HEADS = 8
A_CHUNK = 64
A_MAX_DECAY = 80.0
B_HEADS = 8
B_CONV = 4
B_C = 8.0
C_HDIM = 64
C_HEADS = 16
C_GROUPS = 2
C_STATE = 128
C_CONV = 4
C_CHUNK = 128
C_GROUP_W = WIDTH // C_GROUPS
C_GROUP_HEADS = C_HEADS // C_GROUPS
C_XBLOCKS = C_GROUP_W // LANES
MAIN_COLS = 8 * WIDTH + 2 * C_GROUPS * C_STATE
GATE_COL0 = MAIN_COLS + C_HEADS
GATE_SHIFT = GATE_COL0 - MAIN_COLS
PROJ_TN = 512
PROJ_TILES = MAIN_COLS // PROJ_TN
TILE_BLOCKS = PROJ_TN // LANES
BLK_AQ, BLK_AF, BLK_AI, BLK_AZ = 0, 8, 16, 24
BLK_BX, BLK_BZ = 32, 40
BLK_CZ, BLK_CX, BLK_CB, BLK_CC = 48, 56, 64, 66
MAIN_BLOCKS = MAIN_COLS // LANES
GATE_BLOCKS = D_MODEL // LANES
CONV_HALO = SUBLANES
VMEM_LIMIT = 56 * 1024 * 1024

NEG_LOG2E = -1.4426950408889634
TINY = 1e-30
_NT = (((1,), (1,)), ((), ()))
_TN = (((0,), (0,)), ((), ()))


def _cparams(n_axes):
    return pltpu.CompilerParams(dimension_semantics=("arbitrary",) * n_axes,
                                vmem_limit_bytes=VMEM_LIMIT)


def _sigmoid(x):
    return 1.0 / (1.0 + jnp.exp2(x * NEG_LOG2E))


def _silu(x):
    return x * _sigmoid(x)


def _softplus(x):
    return jnp.maximum(x, 0.0) + jnp.log1p(jnp.exp(-jnp.abs(x)))


def _dot_exact_lhs(m01, x):
    x1 = x.astype(BF16)
    r1 = x - x1.astype(F32)
    x2 = r1.astype(BF16)
    x3 = (r1 - x2.astype(F32)).astype(BF16)
    d = lambda a: jnp.dot(m01, a, preferred_element_type=F32)
    return d(x1) + d(x2) + d(x3)


def _dot_hilo_rhs(x, m01_twice):
    x1 = x.astype(BF16)
    x2 = (x - x1.astype(F32)).astype(BF16)
    return jnp.dot(jnp.concatenate([x1, x2], axis=1), m01_twice, preferred_element_type=F32)


def _tril(n):
    r = lax.broadcasted_iota(jnp.int32, (n, n), 0)
    c = lax.broadcasted_iota(jnp.int32, (n, n), 1)
    return r >= c


def _rms_kernel(x_ref, w_ref, o_ref):
    x = x_ref[...]
    ms = jnp.mean(x * x, axis=-1, keepdims=True)
    o_ref[...] = (x * lax.rsqrt(ms + EPS) * w_ref[...]).astype(o_ref.dtype)


def _rmsnorm(x, w, out_dtype, tm=512):
    t, d = x.shape
    return pl.pallas_call(
        _rms_kernel,
        grid=(t // tm,),
        in_specs=[pl.BlockSpec((tm, d), lambda i: (i, 0)),
                  pl.BlockSpec((1, d), lambda i: (0, 0))],
        out_specs=pl.BlockSpec((tm, d), lambda i: (i, 0)),
        out_shape=jax.ShapeDtypeStruct((t, d), out_dtype),
        compiler_params=_cparams(1),
        name="rmsnorm",
    )(x, w)


def _proj_kernel(h_ref, w_ref, wdt_ref, o_ref, dt_ref):
    h = h_ref[...]
    acc = lax.dot_general(h, w_ref[...].astype(BF16), _NT, preferred_element_type=F32)
    for c in range(TILE_BLOCKS):
        o_ref[c] = acc[:, c * LANES:(c + 1) * LANES].astype(o_ref.dtype)

    @pl.when(pl.program_id(1) == PROJ_TILES - 1)
    def _():
        dt_ref[0] = lax.dot_general(h, wdt_ref[...].astype(BF16), _NT, preferred_element_type=F32)


def _project(h, w_t, layer, tm=2048):
    t, d = h.shape
    return pl.pallas_call(
        _proj_kernel,
        grid=(t // tm, PROJ_TILES),
        in_specs=[pl.BlockSpec((tm, d), lambda i, j: (i, 0)),
                  pl.BlockSpec((None, PROJ_TN, d), lambda i, j: (layer, j, 0)),
                  pl.BlockSpec((None, LANES, d), lambda i, j: (layer, MAIN_COLS // LANES, 0))],
        out_specs=[pl.BlockSpec((TILE_BLOCKS, tm, LANES), lambda i, j: (j, i, 0)),
                   pl.BlockSpec((1, tm, LANES), lambda i, j: (0, i, 0))],
        out_shape=[jax.ShapeDtypeStruct((MAIN_BLOCKS, t, LANES), BF16),
                   jax.ShapeDtypeStruct((1, t, LANES), F32)],
        compiler_params=_cparams(2),
        name="proj_in",
    )(h, w_t, w_t)


def _gate_kernel(h_ref, wa_ref, wb_ref, o_ref):
    w = jnp.concatenate([wa_ref[GATE_SHIFT:, :], wb_ref[...]], axis=0).astype(BF16)
    acc = lax.dot_general(h_ref[...], w, _NT, preferred_element_type=F32)
    for c in range(TILE_BLOCKS):
        o_ref[c] = acc[:, c * LANES:(c + 1) * LANES].astype(o_ref.dtype)


def _project_gates(h, w_t, layer, tm=2048):
    t, d = h.shape
    n_tiles = N_BRANCH * D_MODEL // PROJ_TN
    per_tile = PROJ_TN // GATE_SHIFT
    return pl.pallas_call(
        _gate_kernel,
        grid=(t // tm, n_tiles),
        in_specs=[pl.BlockSpec((tm, d), lambda i, j: (i, 0)),
                  pl.BlockSpec((None, PROJ_TN, d), lambda i, j: (layer, PROJ_TILES + j, 0)),
                  pl.BlockSpec((None, GATE_SHIFT, d), lambda i, j: (layer, (PROJ_TILES + j + 1) * per_tile, 0))],
        out_specs=pl.BlockSpec((TILE_BLOCKS, tm, LANES), lambda i, j: (j, i, 0)),
        out_shape=jax.ShapeDtypeStruct((n_tiles * TILE_BLOCKS, t, LANES), BF16),
        compiler_params=_cparams(2),
        name="proj_gate",
    )(h, w_t, w_t)


def _hgrn_kernel(layer, q_ref, f_ref, i_ref, z_ref, lbl_ref, nw_ref, o_ref, st_ref, b_ref, k_ref, a_ref):
    tt = q_ref.shape[1]

    @pl.when(pl.program_id(2) == 0)
    def _():
        st_ref[...] = jnp.zeros_like(st_ref)

    lbl = lbl_ref[0]
    e = jnp.exp(lbl - jnp.max(lbl, axis=0, keepdims=True))
    lb = jnp.zeros((1, LANES), F32)
    for j in range(1, layer + 1):
        lb = lb + e[j:j + 1, :]
    lb = lb / jnp.sum(e, axis=0, keepdims=True)
    nw = nw_ref[0]

    nc = tt // A_CHUNK
    chunks = [slice(c * A_CHUNK, (c + 1) * A_CHUNK) for c in range(nc)]
    fx = f_ref[0].astype(F32)
    f = lb + (1.0 - lb) * _sigmoid(fx)
    g = jnp.maximum(jnp.log(f), -128.0)
    kk = 1.0 - f
    ngrp = A_CHUNK // SUBLANES
    grouped = (nc, ngrp, SUBLANES, LANES)
    rowm = lax.broadcasted_iota(jnp.int32, grouped, 2)
    b4 = g.reshape(grouped)
    for s in (1, 2, 4):
        b4 = b4 + jnp.where(rowm >= s, pltpu.roll(b4, s, 2), 0.0)
    parts = [b4[:, 0]]
    for j in range(1, ngrp):
        carry = jnp.broadcast_to(parts[-1][:, SUBLANES - 1:SUBLANES, :], (nc, SUBLANES, LANES))
        parts.append(b4[:, j] + carry)
    b = jnp.stack(parts, axis=1).reshape(tt, LANES)
    q = q_ref[0].astype(F32)
    qs = (q * jnp.exp(b)).astype(BF16)
    b_ref[...] = b
    k_ref[...] = kk
    causal = _tril(A_CHUNK)
    small_decay = jnp.min(b) >= -A_MAX_DECAY

    @pl.when(small_decay)
    def _():
        kt = (kk * jnp.exp(-b)).astype(BF16)
        for rows in chunks:
            a = lax.dot_general(qs[rows], kt[rows], _NT, preferred_element_type=F32)
            a_ref[rows, :] = jnp.where(causal, a, 0.0)

    @pl.when(jnp.logical_not(small_decay))
    def _():
        coli = lax.broadcasted_iota(jnp.int32, (A_CHUNK, A_CHUNK), 1)

        def chunk_body(c, carry):
            r0 = pl.multiple_of(c * A_CHUNK, A_CHUNK)
            rows = pl.ds(r0, A_CHUNK)
            bc = b_ref[rows, :]
            qc = q_ref[0, rows, :].astype(F32)

            def col_body(j, acc):
                bj = b_ref[pl.ds(r0 + j, 1), :]
                kj = k_ref[pl.ds(r0 + j, 1), :]
                p = qc * jnp.exp(jnp.minimum(bc - bj, 0.0)) * kj
                return jnp.where(coli == j, jnp.sum(p, axis=-1, keepdims=True), acc)

            acc = lax.fori_loop(0, A_CHUNK, col_body, jnp.zeros((A_CHUNK, A_CHUNK), F32))
            a_ref[rows, :] = jnp.where(causal, acc, 0.0)
            return carry

        lax.fori_loop(0, nc, chunk_body, 0)

    v = i_ref[0]
    b_last = [b[r.stop - 1:r.stop, :] for r in chunks]
    bl_full = jnp.concatenate([jnp.broadcast_to(bl, (A_CHUNK, LANES)) for bl in b_last], axis=0)
    kd = (kk * jnp.exp(bl_full - b)).astype(BF16)
    o_intra = [jnp.dot(a_ref[r, :].astype(BF16), v[r], preferred_element_type=F32) for r in chunks]
    kv = [lax.dot_general(v[r], kd[r], _TN, preferred_element_type=F32) for r in chunks]
    st = st_ref[...]
    outs = []
    for c, r in enumerate(chunks):
        outs.append(o_intra[c] + lax.dot_general(qs[r], st.astype(BF16), _NT, preferred_element_type=F32))
        st = st * jnp.exp(b_last[c]) + kv[c]
    st_ref[...] = st
    o = jnp.concatenate(outs, axis=0)
    ms = jnp.mean(o * o, axis=-1, keepdims=True)
    y = o * lax.rsqrt(ms + EPS) * nw
    o_ref[0] = (y * _silu(z_ref[0].astype(F32))).astype(o_ref.dtype)


def _hgrn2(proj, lbl, nw, layer, tt=2048):
    nt = SEQ // tt
    blk = lambda off: pl.BlockSpec((1, tt, LANES), lambda b, hd, t: (off + hd, b * nt + t, 0))
    return pl.pallas_call(
        functools.partial(_hgrn_kernel, layer),
        grid=(BATCH, A_HEADS, nt),
        in_specs=[blk(BLK_AQ), blk(BLK_AF), blk(BLK_AI), blk(BLK_AZ),
                  pl.BlockSpec((1, DEPTH, LANES), lambda b, hd, t: (hd, 0, 0)),
                  pl.BlockSpec((None, 1, 1, LANES), lambda b, hd, t: (layer, hd, 0, 0))],
        out_specs=pl.BlockSpec((1, tt, LANES), lambda b, hd, t: (hd, b * nt + t, 0)),
        out_shape=jax.ShapeDtypeStruct((A_HEADS, TOKENS, LANES), BF16),
        scratch_shapes=[pltpu.VMEM((LANES, LANES), F32),
                        pltpu.VMEM((tt, LANES), F32),
                        pltpu.VMEM((tt, LANES), F32),
                        pltpu.VMEM((tt, A_CHUNK), F32)],
        compiler_params=_cparams(3),
        name="hgrn2",
    )(proj, proj, proj, proj, lbl, nw)


def _causal_conv(xbuf, x, cw, cb, first):
    tt = x.shape[0]

    @pl.when(first)
    def _():
        xbuf[0:CONV_HALO, :] = jnp.zeros((CONV_HALO, LANES), F32)

    xbuf[CONV_HALO:CONV_HALO + tt, :] = x
    y = cb
    for k in range(B_CONV):
        y = y + cw[k:k + 1, :] * xbuf[pl.ds(CONV_HALO - (B_CONV - 1) + k, tt), :]
    xbuf[0:CONV_HALO, :] = x[tt - CONV_HALO:tt, :]
    return y


def _rglru_kernel(x_ref, z_ref, cw_ref, cb_ref, wa_ref, ba_ref, wx_ref, bx_ref, lam_ref, o_ref,
                  xbuf, a_s, u_s, hcar):
    tt = x_ref.shape[1]
    first = pl.program_id(1) == 0

    @pl.when(first)
    def _():
        hcar[...] = jnp.zeros_like(hcar)

    grouped = (tt // SUBLANES, SUBLANES, LANES)
    rowm = lax.broadcasted_iota(jnp.int32, grouped, 1)
    for h in range(B_HEADS):
        xc = _causal_conv(xbuf.at[h], x_ref[h].astype(F32), cw_ref[h], cb_ref[h], first)
        xcb = xc.astype(BF16)
        r = _sigmoid(jnp.dot(xcb, wa_ref[h], preferred_element_type=F32) + ba_ref[h])
        ig = _sigmoid(jnp.dot(xcb, wx_ref[h], preferred_element_type=F32) + bx_ref[h])
        log_a = (-B_C) * r * _softplus(-lam_ref[h])
        a = jnp.exp(log_a)
        y = 1.0 - jnp.exp(2.0 * log_a)
        u = (y * lax.rsqrt(jnp.maximum(y, TINY))) * (ig * xc)
        a = a.reshape(grouped)
        u = u.reshape(grouped)
        for s in (1, 2, 4):
            a_sh = pltpu.roll(a, s, 1)
            u_sh = pltpu.roll(u, s, 1)
            m = rowm >= s
            u = jnp.where(m, a * u_sh + u, u)
            a = jnp.where(m, a * a_sh, a)
        a_s[h] = a.reshape(tt, LANES)
        u_s[h] = u.reshape(tt, LANES)

    def grp(g, hs):
        rows = pl.ds(pl.multiple_of(g * SUBLANES, SUBLANES), SUBLANES)
        new = []
        for h in range(B_HEADS):
            hg = a_s[h, rows, :] * hs[h] + u_s[h, rows, :]
            u_s[h, rows, :] = hg
            new.append(jnp.broadcast_to(hg[SUBLANES - 1:SUBLANES, :], (SUBLANES, LANES)))
        return tuple(new)

    hs = lax.fori_loop(0, tt // SUBLANES, grp, tuple(hcar[h] for h in range(B_HEADS)), unroll=4)
    for h in range(B_HEADS):
        hcar[h] = hs[h]
        o_ref[h] = (u_s[h] * _silu(z_ref[h].astype(F32))).astype(o_ref.dtype)


def _rglru(proj, cw, cb, wa, ba, wx, bx, lam, layer, tt=512):
    nt = SEQ // tt
    rows = lambda b, t: b * nt + t
    blk = lambda off: pl.BlockSpec((B_HEADS, tt, LANES), lambda b, t: (off // B_HEADS, rows(b, t), 0))
    par = lambda r, c: pl.BlockSpec((None, B_HEADS, r, c), lambda b, t: (layer, 0, 0, 0))
    return pl.pallas_call(
        _rglru_kernel,
        grid=(BATCH, nt),
        in_specs=[blk(BLK_BX), blk(BLK_BZ), par(B_CONV, LANES), par(1, LANES),
                  par(LANES, LANES), par(1, LANES), par(LANES, LANES), par(1, LANES), par(1, LANES)],
        out_specs=pl.BlockSpec((B_HEADS, tt, LANES), lambda b, t: (0, rows(b, t), 0)),
        out_shape=jax.ShapeDtypeStruct((B_HEADS, TOKENS, LANES), BF16),
        scratch_shapes=[pltpu.VMEM((B_HEADS, tt + CONV_HALO, LANES), F32),
                        pltpu.VMEM((B_HEADS, tt, LANES), F32),
                        pltpu.VMEM((B_HEADS, tt, LANES), F32),
                        pltpu.VMEM((B_HEADS, SUBLANES, LANES), F32)],
        compiler_params=_cparams(2),
        name="rglru",
    )(proj, proj, cw, cb, wa, ba, wx, bx, lam)


def _ssd_kernel(z_ref, x_ref, bm_ref, cm_ref, dt_ref, cwx_ref, cwb_ref, cwc_ref, cbx_ref, cbb_ref, cbc_ref,
                dtb_ref, alog_ref, dsk_ref, nw_ref, o_ref, xbuf, st_ref):
    tt = x_ref.shape[1]
    nxb = C_XBLOCKS
    first = pl.program_id(1) == 0

    @pl.when(first)
    def _():
        st_ref[...] = jnp.zeros_like(st_ref)

    dt_all = _softplus(dt_ref[0] + dtb_ref[...])
    da_all = dt_all * (-jnp.exp(alog_ref[...]))
    causal = _tril(C_CHUNK)
    causal_b = causal.astype(BF16)
    lane = lax.broadcasted_iota(jnp.int32, (C_CHUNK, LANES), 1)
    er = lax.broadcasted_iota(jnp.int32, (2 * LANES, C_GROUP_W), 0) & (LANES - 1)
    ec = lax.broadcasted_iota(jnp.int32, (2 * LANES, C_GROUP_W), 1)

    xs_all, bm_all, cm_all, spreads, dsk, nw, st = [], [], [], [], [], [], []
    for g in range(C_GROUPS):
        xs_all.append([_silu(_causal_conv(xbuf.at[g * nxb + p], x_ref[g * nxb + p].astype(F32),
                                          cwx_ref[g * nxb + p], cbx_ref[g * nxb + p], first)) for p in range(nxb)])
        bm_all.append(_silu(_causal_conv(xbuf.at[2 * nxb + g], bm_ref[g].astype(F32), cwb_ref[g], cbb_ref[g],
                                         first)).astype(BF16))
        cm_all.append(_silu(_causal_conv(xbuf.at[2 * nxb + C_GROUPS + g], cm_ref[g].astype(F32), cwc_ref[g],
                                         cbc_ref[g], first)).astype(BF16))
        ej = er - g * C_GROUP_HEADS
        spreads.append(((ec >= ej * C_HDIM) & (ec < (ej + 1) * C_HDIM)).astype(BF16))
        dsk.append(jnp.concatenate([dsk_ref[g * nxb + p] for p in range(nxb)], axis=1))
        nw.append(jnp.concatenate([nw_ref[g * nxb + p] for p in range(nxb)], axis=1))
        st.append(st_ref[g])

    for c in range(tt // C_CHUNK):
        rows = slice(c * C_CHUNK, (c + 1) * C_CHUNK)
        dt = dt_all[rows]
        cum = _dot_exact_lhs(causal_b, da_all[rows])
        cum_t = cum.T
        dt_t = dt.T
        ecum = jnp.exp(cum)
        cl = cum[C_CHUNK - 1:C_CHUNK, :]
        w = dt * jnp.exp(cl - cum)
        ecl = jnp.broadcast_to(jnp.exp(cl), (SUBLANES, LANES))
        for g in range(C_GROUPS):
            cmb = cm_all[g][rows]
            bmb = bm_all[g][rows]
            cb = lax.dot_general(cmb, bmb, _NT, preferred_element_type=F32)
            y_blocks = []
            for p in range(nxb):
                xb = xs_all[g][p][rows]
                acc = None
                for half in range(2):
                    hl = g * C_GROUP_HEADS + 2 * p + half
                    seg = cum[:, hl:hl + 1] - cum_t[hl:hl + 1, :]
                    lmat = jnp.exp(jnp.where(causal, seg, -1e30))
                    m = (cb * lmat * dt_t[hl:hl + 1, :]).astype(BF16)
                    in_head = (lane >= C_HDIM) if half else (lane < C_HDIM)
                    rhs = jnp.where(in_head, xb, 0.0).astype(BF16)
                    part = jnp.dot(m, rhs, preferred_element_type=F32)
                    acc = part if acc is None else acc + part
                y_blocks.append(acc)
            xs = jnp.concatenate([xs_all[g][p][rows] for p in range(nxb)], axis=1)
            y = jnp.concatenate(y_blocks, axis=1)
            y = y + (jnp.dot(cmb, st[g].astype(BF16), preferred_element_type=F32)
                     * _dot_hilo_rhs(ecum, spreads[g]))
            y = y + dsk[g] * xs
            xw = (xs * _dot_hilo_rhs(w, spreads[g])).astype(BF16)
            dec = _dot_hilo_rhs(ecl, spreads[g])[0:1, :]
            st[g] = st[g] * dec + lax.dot_general(bmb, xw, _TN, preferred_element_type=F32)
            zf = jnp.concatenate([z_ref[g * nxb + p, rows, :].astype(F32) for p in range(nxb)], axis=1)
            yg = y * _silu(zf)
            ms = jnp.mean(yg * yg, axis=-1, keepdims=True)
            out = yg * lax.rsqrt(ms + EPS) * nw[g]
            for p in range(nxb):
                o_ref[g * nxb + p, rows, :] = out[:, p * LANES:(p + 1) * LANES].astype(o_ref.dtype)

    for g in range(C_GROUPS):
        st_ref[g] = st[g]


def _ssd(proj, dtp, cw, cb, dtb, alog, dsk, nw, layer, tt=8 * C_CHUNK):
    nt = SEQ // tt
    nx = C_GROUPS * C_XBLOCKS
    rows = lambda b, t: b * nt + t
    big = lambda off: pl.BlockSpec((nx, tt, LANES), lambda b, t: (off // nx, rows(b, t), 0))
    two = lambda off: pl.BlockSpec((C_GROUPS, tt, LANES), lambda b, t: (off // C_GROUPS, rows(b, t), 0))
    par = lambda n, r, blk: pl.BlockSpec((None, n, r, LANES), lambda b, t: (layer, blk, 0, 0))
    vec = pl.BlockSpec((None, 1, LANES), lambda b, t: (layer, 0, 0))
    return pl.pallas_call(
        _ssd_kernel,
        grid=(BATCH, nt),
        in_specs=[big(BLK_CZ), big(BLK_CX), two(BLK_CB), two(BLK_CC),
                  pl.BlockSpec((1, tt, LANES), lambda b, t: (0, rows(b, t), 0)),
                  par(nx, C_CONV, 0), par(C_GROUPS, C_CONV, nx // C_GROUPS), par(C_GROUPS, C_CONV, nx // C_GROUPS + 1),
                  par(nx, 1, 0), par(C_GROUPS, 1, nx // C_GROUPS), par(C_GROUPS, 1, nx // C_GROUPS + 1),
                  vec, vec, par(nx, 1, 0), par(nx, 1, 0)],
        out_specs=pl.BlockSpec((nx, tt, LANES), lambda b, t: (0, rows(b, t), 0)),
        out_shape=jax.ShapeDtypeStruct((nx, TOKENS, LANES), BF16),
        scratch_shapes=[pltpu.VMEM((nx + 2 * C_GROUPS, tt + CONV_HALO, LANES), F32),
                        pltpu.VMEM((C_GROUPS, C_STATE, C_GROUP_W), F32)],
        compiler_params=_cparams(2),
        name="ssd",
    )(proj, proj, proj, proj, dtp, cw, cw, cw, cb, cb, cb, dtb, alog, dsk, nw)


def _merge_kernel(final, oa_ref, ob_ref, oc_ref, g_ref, x_ref, w_ref, nw_ref, *out_refs):
    nb = WIDTH // LANES
    acc = x_ref[...]
    for br, o_ref in enumerate((oa_ref, ob_ref, oc_ref)):
        o = jnp.concatenate([o_ref[hh] for hh in range(nb)], axis=1)
        bo = jnp.dot(o, w_ref[br], preferred_element_type=F32)
        gl = jnp.concatenate([g_ref[br * GATE_BLOCKS + c] for c in range(GATE_BLOCKS)], axis=1).astype(F32)
        acc = acc + _sigmoid(gl) * bo
    ms = jnp.mean(acc * acc, axis=-1, keepdims=True)
    hn = acc * lax.rsqrt(ms + EPS) * nw_ref[...]
    if final:
        out_refs[0][...] = hn.astype(out_refs[0].dtype)
    else:
        out_refs[0][...] = acc
        out_refs[1][...] = hn.astype(out_refs[1].dtype)


def _merge(oa, ob, oc, gates, x, w_out, nw, layer, final, tm=256):
    nb = WIDTH // LANES
    obs = pl.BlockSpec((nb, tm, LANES), lambda i: (0, i, 0))
    gbs = pl.BlockSpec((N_BRANCH * GATE_BLOCKS, tm, LANES), lambda i: (0, i, 0))
    row = pl.BlockSpec((tm, D_MODEL), lambda i: (i, 0))
    if final:
        out_specs = row
        out_shape = jax.ShapeDtypeStruct((TOKENS, D_MODEL), F32)
    else:
        out_specs = [row, row]
        out_shape = [jax.ShapeDtypeStruct((TOKENS, D_MODEL), F32),
                     jax.ShapeDtypeStruct((TOKENS, D_MODEL), BF16)]
    return pl.pallas_call(
        functools.partial(_merge_kernel, final),
        grid=(TOKENS // tm,),
        in_specs=[obs, obs, obs, gbs, row,
                  pl.BlockSpec((None, N_BRANCH, WIDTH, D_MODEL), lambda i: (layer, 0, 0, 0)),
                  pl.BlockSpec((1, D_MODEL), lambda i: (0, 0))],
        out_specs=out_specs,
        out_shape=out_shape,
        compiler_params=_cparams(1),
        name="merge",
    )(oa, ob, oc, gates, x, w_out, nw)


def _blocks(p, rows):
    d, r, n = p.shape
    return p.reshape(d, r, n // LANES, LANES).transpose(0, 2, 1, 3)


def _pad_lanes(p):
    return jnp.pad(p, ((0, 0),) * (p.ndim - 1) + ((0, LANES - p.shape[-1]),))


def kernel(x, norm_w, w_in, hgrn_lb_logits, hgrn_norm_w, rglru_conv_w, rglru_conv_b, rglru_wa, rglru_ba,
           rglru_wx, rglru_bx, rglru_lambda, ssd_conv_w, ssd_conv_b, ssd_dt_bias, ssd_a_log, ssd_d,
           ssd_norm_w, w_out, final_norm_w):
    w_t = jnp.swapaxes(w_in, 1, 2)
    w_o = w_out.reshape(DEPTH, N_BRANCH, WIDTH, D_MODEL).astype(BF16)
    lbl = hgrn_lb_logits.reshape(DEPTH, A_HEADS, LANES).transpose(1, 0, 2)
    a_nw = _blocks(hgrn_norm_w[:, None, :], 1)
    b_cw = _blocks(rglru_conv_w, B_CONV)
    b_cb = _blocks(rglru_conv_b[:, None, :], 1)
    b_wa = rglru_wa.astype(BF16)
    b_wx = rglru_wx.astype(BF16)
    b_ba = rglru_ba[:, :, None, :]
    b_bx = rglru_bx[:, :, None, :]
    b_lam = _blocks(rglru_lambda[:, None, :], 1)
    c_cw = _blocks(ssd_conv_w, C_CONV)
    c_cb = _blocks(ssd_conv_b[:, None, :], 1)
    c_dtb = _pad_lanes(ssd_dt_bias)[:, None, :]
    c_alog = _pad_lanes(ssd_a_log)[:, None, :]
    c_dsk = _blocks(jnp.repeat(ssd_d, C_HDIM, axis=1)[:, None, :], 1)
    c_nw = _blocks(ssd_norm_w[:, None, :], 1)

    xf = x.reshape(TOKENS, D_MODEL)
    h = _rmsnorm(xf, norm_w[0:1], BF16)
    for l in range(DEPTH):
        proj, dtp = _project(h, w_t, l)
        gates = _project_gates(h, w_t, l)
        oa = _hgrn2(proj, lbl, a_nw, l)
        ob = _rglru(proj, b_cw, b_cb, b_wa, b_ba, b_wx, b_bx, b_lam, l)
        oc = _ssd(proj, dtp, c_cw, c_cb, c_dtb, c_alog, c_dsk, c_nw, l)
        if l + 1 < DEPTH:
            xf, h = _merge(oa, ob, oc, gates, xf, w_o, norm_w[l + 1:l + 2], l, final=False)
        else:
            out = _merge(oa, ob, oc, gates, xf, w_o, final_norm_w[None, :], l, final=True)
    return out.reshape(BATCH, SEQ, D_MODEL)
```
